```python
import math
import jax, jax.numpy as jnp
from jax import lax
import numpy as np

D_MODEL = 1024
BATCH = 32
SEQ = 2048
DEPTH = 1

D_MIX = D_MODEL
D_S5 = D_MIX // 2
D_ML = D_MIX - D_S5
S5_GROUP = 16
S5_NG = D_S5 // S5_GROUP
S5_P = 64
ML_HEADS = 4
ML_DH = D_ML // ML_HEADS
ML_CHUNK = 128
CONV_K = 5
N_GATES = 4
D_IN = D_S5 + 2 * D_ML + N_GATES * ML_HEADS
D_FF = 256 * math.ceil(8 * D_MODEL / 3 / 256)
EPS = 1e-6

kernel_name = "hybrid_s5_mlstm_bidir_block"


def rmsnorm(x, g):
    x32 = x.astype(jnp.float32)
    y = x32 * lax.rsqrt(jnp.mean(x32 * x32, axis=-1, keepdims=True) + EPS)
    return (y * g.astype(jnp.float32)).astype(x.dtype)


def s5_discretise(lam_re, lam_im, log_dt, b_re, b_im):
    f32 = jnp.float32
    lr, li = lam_re.astype(f32), lam_im.astype(f32)
    dt = jnp.exp(log_dt.astype(f32))[:, None]
    mag = jnp.exp(lr * dt)
    bar_re, bar_im = mag * jnp.cos(li * dt), mag * jnp.sin(li * dt)
    xr, xi = bar_re - 1.0, bar_im
    den = lr * lr + li * li
    fr = (xr * lr + xi * li) / den
    fi = (xi * lr - xr * li) / den
    br, bi = b_re.astype(f32), b_im.astype(f32)
    bb_re = fr[..., None] * br - fi[..., None] * bi
    bb_im = fr[..., None] * bi + fi[..., None] * br
    return bar_re, bar_im, bb_re, bb_im


def _linear_recurrence_combine(e1, e2):
    a1r, a1i, b1r, b1i = e1
    a2r, a2i, b2r, b2i = e2
    return (a2r * a1r - a2i * a1i,
            a2r * a1i + a2i * a1r,
            a2r * b1r - a2i * b1i + b2r,
            a2r * b1i + a2i * b1r + b2i)


def s5_scan(u, lam_re, lam_im, log_dt, b_re, b_im, reverse):
    bar_re, bar_im, bb_re, bb_im = s5_discretise(lam_re, lam_im, log_dt, b_re, b_im)
    bu_re = jnp.einsum('bsgc,gpc->bsgp', u, bb_re)
    bu_im = jnp.einsum('bsgc,gpc->bsgp', u, bb_im)
    a_re = jnp.broadcast_to(bar_re, bu_re.shape)
    a_im = jnp.broadcast_to(bar_im, bu_im.shape)
    _, _, s_re, s_im = lax.associative_scan(
        _linear_recurrence_combine, (a_re, a_im, bu_re, bu_im), axis=1, reverse=reverse)
    return s_re, s_im


def s5_group(u, lam_re, lam_im, log_dt, b_re, b_im, c_re, c_im, d, w_glu, b_glu):
    bsz, seq, _ = u.shape
    f32 = jnp.float32
    u32 = u.astype(f32)
    ug = u32.reshape(bsz, seq, S5_NG, S5_GROUP)
    sf_re, sf_im = s5_scan(ug, lam_re[0], lam_im[0], log_dt[0], b_re, b_im, False)
    sb_re, sb_im = s5_scan(ug, lam_re[1], lam_im[1], log_dt[1], b_re, b_im, True)
    s_re, s_im = sf_re + sb_re, sf_im + sb_im
    y = (jnp.einsum('bsgp,gcp->bsgc', s_re, c_re.astype(f32))
         - jnp.einsum('bsgp,gcp->bsgc', s_im, c_im.astype(f32)))
    y = y.reshape(bsz, seq, D_S5) + d.astype(f32) * u32
    y = jax.nn.gelu(y)
    out = y * jax.nn.sigmoid(y @ w_glu.astype(f32) + b_glu.astype(f32))
    return out.astype(u.dtype)


def _mlstm_chunk_step(carry, inp):
    c_state, n_state, m_state = carry
    q, k, v, li, lf = inp
    L = q.shape[2]
    b = jnp.cumsum(lf, axis=-1)
    a = b + m_state[..., None]
    lower_tri = jnp.tril(jnp.ones((L, L), dtype=bool))
    d = jnp.where(lower_tri, b[..., :, None] - b[..., None, :] + li[..., None, :], -jnp.inf)
    m_t = jnp.maximum(a, jnp.max(d, axis=-1))
    w_intra = jnp.exp(d - m_t[..., None])
    w_inter = jnp.exp(a - m_t)
    s = jnp.einsum('bhtd,bhsd->bhts', q, k) * w_intra
    num = (jnp.einsum('bhts,bhsd->bhtd', s, v)
           + w_inter[..., None] * jnp.einsum('bhvk,bhtk->bhtv', c_state, q))
    den = jnp.sum(s, axis=-1) + w_inter * jnp.einsum('bhk,bhtk->bht', n_state, q)
    h = num / jnp.maximum(jnp.abs(den), jnp.exp(-m_t))[..., None]
    b_last = b[..., -1]
    g = b_last[..., None] - b + li
    m_new = jnp.maximum(b_last + m_state, jnp.max(g, axis=-1))
    decay = jnp.exp(b_last + m_state - m_new)
    w_state = jnp.exp(g - m_new[..., None])
    c_new = decay[..., None, None] * c_state + jnp.einsum('bhs,bhsv,bhsk->bhvk', w_state, v, k)
    n_new = decay[..., None] * n_state + jnp.einsum('bhs,bhsk->bhk', w_state, k)
    return (c_new, n_new, m_new), h


def mlstm_chunkwise(q, k, v, i_pre, f_pre):
    bsz, nh, seq, dh = q.shape
    nc = seq // ML_CHUNK
    li = i_pre
    lf = jax.nn.log_sigmoid(f_pre)

    def to_chunks(t):
        return jnp.moveaxis(t.reshape(bsz, nh, nc, ML_CHUNK, *t.shape[3:]), 2, 0)

    xs = (to_chunks(q), to_chunks(k), to_chunks(v), to_chunks(li), to_chunks(lf))
    init = (jnp.zeros((bsz, nh, dh, dh), jnp.float32),
            jnp.zeros((bsz, nh, dh), jnp.float32),
            jnp.zeros((bsz, nh), jnp.float32))
    _, hs = lax.scan(_mlstm_chunk_step, init, xs)
    return jnp.moveaxis(hs, 0, 2).reshape(bsz, nh, seq, dh)


def _conv_centred(x, w, b):
    y = lax.conv_general_dilated(
        x, w[:, None, :].astype(x.dtype), window_strides=(1,),
        padding=[(CONV_K // 2, CONV_K // 2)],
        dimension_numbers=('NWC', 'WIO', 'NWC'),
        feature_group_count=x.shape[-1])
    return y + b.astype(x.dtype)


def mlstm_group(x_m, o_pre, gate_pre, gate_bias, conv_w, conv_b, wq, wk, wv, head_norm, skip):
    bsz, seq, _ = x_m.shape
    f32 = jnp.float32
    xc = jax.nn.silu(_conv_centred(x_m, conv_w, conv_b))
    xc_h = xc.astype(f32).reshape(bsz, seq, ML_HEADS, ML_DH)
    xm_h = x_m.astype(f32).reshape(bsz, seq, ML_HEADS, ML_DH)
    q = jnp.einsum('bshd,hde->bhse', xc_h, wq.astype(f32))
    k = jnp.einsum('bshd,hde->bhse', xc_h, wk.astype(f32)) * (ML_DH ** -0.5)
    v = jnp.einsum('bshd,hde->bhse', xm_h, wv.astype(f32))
    gp = jnp.transpose(gate_pre.astype(f32) + gate_bias.astype(f32), (2, 0, 3, 1))
    h_fwd = mlstm_chunkwise(q, k, v, gp[0], gp[2])
    flip = lambda t: jnp.flip(t, axis=2)
    h_bwd = flip(mlstm_chunkwise(flip(q), flip(k), flip(v), flip(gp[1]), flip(gp[3])))
    h = h_fwd + h_bwd
    mu = jnp.mean(h, axis=-1, keepdims=True)
    var = jnp.mean(jnp.square(h - mu), axis=-1, keepdims=True)
    hn = (h - mu) * lax.rsqrt(var + EPS)
    hn = jnp.transpose(hn, (0, 2, 1, 3)).reshape(bsz, seq, D_ML) * head_norm.astype(f32)
    out = jax.nn.sigmoid(o_pre.astype(f32)) * (hn + skip.astype(f32) * xc.astype(f32))
    return out.astype(x_m.dtype)


def hybrid_layer(x, norm_mix_pre, norm_mix_post, norm_ffn_pre, norm_ffn_post, w_in, ml_gate_bias,
                 s5_lam_re, s5_lam_im, s5_log_dt, s5_b_re, s5_b_im, s5_c_re, s5_c_im, s5_d,
                 s5_w_glu, s5_b_glu, ml_conv_w, ml_conv_b, ml_wq, ml_wk, ml_wv, ml_head_norm,
                 ml_skip, w_out, w_gate, w_up, w_down):
    bsz, seq, _ = x.shape
    h = rmsnorm(x, norm_mix_pre)
    proj = h @ w_in.astype(h.dtype)
    u_s5 = proj[..., :D_S5]
    x_m = proj[..., D_S5:D_S5 + D_ML]
    o_pre = proj[..., D_S5 + D_ML:D_S5 + 2 * D_ML]
    gate_pre = proj[..., D_S5 + 2 * D_ML:].reshape(bsz, seq, N_GATES, ML_HEADS)
    y_s5 = s5_group(u_s5, s5_lam_re, s5_lam_im, s5_log_dt, s5_b_re, s5_b_im,
                    s5_c_re, s5_c_im, s5_d, s5_w_glu, s5_b_glu)
    y_ml = mlstm_group(x_m, o_pre, gate_pre, ml_gate_bias, ml_conv_w, ml_conv_b,
                       ml_wq, ml_wk, ml_wv, ml_head_norm, ml_skip)
    y = jnp.concatenate([y_s5, y_ml], axis=-1) @ w_out.astype(x.dtype)
    x = x + rmsnorm(y, norm_mix_post)
    h = rmsnorm(x, norm_ffn_pre)
    f = (jax.nn.silu(h @ w_gate.astype(h.dtype)) * (h @ w_up.astype(h.dtype))) @ w_down.astype(h.dtype)
    return x + rmsnorm(f, norm_ffn_post)


def setup_inputs(seed: int = 0) -> dict:
    key = jax.random.key(seed)
    ks = jax.random.split(key, 32)
    f32 = jnp.float32
    nrm = lambda k, shape, scale: jax.random.normal(k, shape, f32) * scale
    gain = lambda k, n: 1.0 + 0.05 * jax.random.normal(k, (DEPTH, n), f32)
    x = jax.random.normal(ks[0], (BATCH, SEQ, D_MODEL), f32)
    gb_noise = 0.1 * jax.random.normal(ks[6], (DEPTH, N_GATES, ML_HEADS), f32)
    f_lin = jnp.linspace(3.0, 6.0, ML_HEADS, dtype=f32)
    gate_offset = jnp.stack([jnp.zeros_like(f_lin), jnp.zeros_like(f_lin), f_lin, f_lin], axis=0)
    ml_gate_bias = gb_noise + gate_offset[None]
    n_idx = jnp.arange(S5_P, dtype=f32)
    s5_lam_re = -0.5 + 0.01 * jax.random.normal(ks[7], (DEPTH, 2, S5_NG, S5_P), f32)
    s5_lam_im = math.pi * n_idx + 0.01 * jax.random.normal(ks[8], (DEPTH, 2, S5_NG, S5_P), f32)
    s5_log_dt = jax.random.uniform(ks[9], (DEPTH, 2, S5_NG), f32,
                                   minval=math.log(1e-3), maxval=math.log(1e-1))
    return {
        'x': x,
        'norm_mix_pre': gain(ks[1], D_MODEL),
        'norm_mix_post': gain(ks[2], D_MODEL),
        'norm_ffn_pre': gain(ks[3], D_MODEL),
        'norm_ffn_post': gain(ks[4], D_MODEL),
        'w_in': nrm(ks[5], (DEPTH, D_MODEL, D_IN), D_MODEL ** -0.5),
        'ml_gate_bias': ml_gate_bias,
        's5_lam_re': s5_lam_re,
        's5_lam_im': s5_lam_im,
        's5_log_dt': s5_log_dt,
        's5_b_re': nrm(ks[10], (DEPTH, S5_NG, S5_P, S5_GROUP), (2 * S5_GROUP) ** -0.5),
        's5_b_im': nrm(ks[11], (DEPTH, S5_NG, S5_P, S5_GROUP), (2 * S5_GROUP) ** -0.5),
        's5_c_re': nrm(ks[12], (DEPTH, S5_NG, S5_GROUP, S5_P), (2 * S5_P) ** -0.5),
        's5_c_im': nrm(ks[13], (DEPTH, S5_NG, S5_GROUP, S5_P), (2 * S5_P) ** -0.5),
        's5_d': nrm(ks[14], (DEPTH, D_S5), 1.0),
        's5_w_glu': nrm(ks[15], (DEPTH, D_S5, D_S5), D_S5 ** -0.5),
        's5_b_glu': nrm(ks[16], (DEPTH, D_S5), 0.02),
        'ml_conv_w': nrm(ks[17], (DEPTH, CONV_K, D_ML), CONV_K ** -0.5),
        'ml_conv_b': nrm(ks[18], (DEPTH, D_ML), 0.02),
        'ml_wq': nrm(ks[19], (DEPTH, ML_HEADS, ML_DH, ML_DH), ML_DH ** -0.5),
        'ml_wk': nrm(ks[20], (DEPTH, ML_HEADS, ML_DH, ML_DH), ML_DH ** -0.5),
        'ml_wv': nrm(ks[21], (DEPTH, ML_HEADS, ML_DH, ML_DH), ML_DH ** -0.5),
        'ml_head_norm': gain(ks[22], D_ML),
        'ml_skip': 1.0 + 0.05 * jax.random.normal(ks[23], (DEPTH, D_ML), f32),
        'w_out': nrm(ks[24], (DEPTH, D_MIX, D_MODEL), D_MIX ** -0.5),
        'w_gate': nrm(ks[25], (DEPTH, D_MODEL, D_FF), D_MODEL ** -0.5),
        'w_up': nrm(ks[26], (DEPTH, D_MODEL, D_FF), D_MODEL ** -0.5),
        'w_down': nrm(ks[27], (DEPTH, D_FF, D_MODEL), D_FF ** -0.5),
    }


def reference(x, norm_mix_pre, norm_mix_post, norm_ffn_pre, norm_ffn_post, w_in, ml_gate_bias,
              s5_lam_re, s5_lam_im, s5_log_dt, s5_b_re, s5_b_im, s5_c_re, s5_c_im, s5_d,
              s5_w_glu, s5_b_glu, ml_conv_w, ml_conv_b, ml_wq, ml_wk, ml_wv, ml_head_norm,
              ml_skip, w_out, w_gate, w_up, w_down):
    for l in range(DEPTH):
        x = hybrid_layer(
            x, norm_mix_pre[l], norm_mix_post[l], norm_ffn_pre[l], norm_ffn_post[l], w_in[l],
            ml_gate_bias[l], s5_lam_re[l], s5_lam_im[l], s5_log_dt[l], s5_b_re[l], s5_b_im[l],
            s5_c_re[l], s5_c_im[l], s5_d[l], s5_w_glu[l], s5_b_glu[l], ml_conv_w[l], ml_conv_b[l],
            ml_wq[l], ml_wk[l], ml_wv[l], ml_head_norm[l], ml_skip[l], w_out[l], w_gate[l],
            w_up[l], w_down[l])
    return x
```

```python
import functools
import math

import jax
import jax.numpy as jnp
from jax import lax
from jax.experimental import pallas as pl
from jax.experimental.pallas import tpu as pltpu

D_MODEL = 1024
D_S5 = 512
D_ML = 512
S5_GROUP = 16
S5_NG = D_S5 // S5_GROUP
S5_P = 64
ML_HEADS = 4
ML_DH = D_ML // ML_HEADS
CONV_K = 5
N_GATES = 4
N_GATE_COLS = N_GATES * ML_HEADS
D_FF = 256 * math.ceil(8 * D_MODEL / 3 / 256)
EPS = 1e-6

LANES = 128
S5_CHUNK = LANES
ML_CHUNK = 128
FF_CHUNK = 256
VMEM_LIMIT = 56 * 1024 * 1024

F32 = jnp.float32
BF16 = jnp.bfloat16

_NT = (((1,), (1,)), ((), ()))
_TN = (((0,), (0,)), ((), ()))


def _rms(x):
    return x * lax.rsqrt(jnp.mean(x * x, axis=-1, keepdims=True) + EPS)


def _sigmoid(x):
    return 1.0 / (1.0 + jnp.exp(-x))


def _gelu_tanh(x):
    c = math.sqrt(2.0 / math.pi)
    return 0.5 * x * (1.0 + jnp.tanh(c * (x + 0.044715 * (x * x * x))))


def _params(**kw):
    return pltpu.CompilerParams(vmem_limit_bytes=VMEM_LIMIT, **kw)


def _inproj_kernel(x_ref, g_ref, wt_ref, w_ref, ut_ref, gt_ref, xm_ref, op_ref):
    x = x_ref[...]
    h = (_rms(x) * g_ref[...]).astype(BF16)
    st = lax.dot_general(wt_ref[...], h, _NT, preferred_element_type=F32)
    ut_ref[...] = st[:D_S5].astype(BF16)
    gt_ref[...] = st[D_S5:]
    p = jnp.dot(h, w_ref[...], preferred_element_type=F32)
    xm_ref[...] = p[:, :D_ML].astype(BF16)
    op_ref[...] = p[:, D_ML:].astype(BF16)


def _inproj(x2, g_pre, w_in, tm):
    T = x2.shape[0]
    wt = jnp.concatenate([w_in[:, :D_S5], w_in[:, D_S5 + 2 * D_ML:]], axis=1).T.astype(BF16)
    w = w_in[:, D_S5:D_S5 + 2 * D_ML].astype(BF16)
    n_t = D_S5 + N_GATE_COLS
    return pl.pallas_call(
        _inproj_kernel,
        grid=(T // tm,),
        in_specs=[
            pl.BlockSpec((tm, D_MODEL), lambda i: (i, 0)),
            pl.BlockSpec((1, D_MODEL), lambda i: (0, 0)),
            pl.BlockSpec((n_t, D_MODEL), lambda i: (0, 0)),
            pl.BlockSpec((D_MODEL, 2 * D_ML), lambda i: (0, 0)),
        ],
        out_specs=[
            pl.BlockSpec((D_S5, tm), lambda i: (0, i)),
            pl.BlockSpec((N_GATE_COLS, tm), lambda i: (0, i)),
            pl.BlockSpec((tm, D_ML), lambda i: (i, 0)),
            pl.BlockSpec((tm, D_ML), lambda i: (i, 0)),
        ],
        out_shape=[
            jax.ShapeDtypeStruct((D_S5, T), BF16),
            jax.ShapeDtypeStruct((N_GATE_COLS, T), F32),
            jax.ShapeDtypeStruct((T, D_ML), BF16),
            jax.ShapeDtypeStruct((T, D_ML), BF16),
        ],
        compiler_params=_params(dimension_semantics=("arbitrary",)),
        name="inproj",
    )(x2, g_pre.reshape(1, D_MODEL), wt, w)


def _s5_weights(lam_re, lam_im, log_dt, b_re, b_im, c_re, c_im, d, nc):
    L, G, P, C = S5_CHUNK, S5_NG, S5_P, S5_GROUP
    hi = lax.Precision.HIGHEST
    lr, li = lam_re.astype(F32), lam_im.astype(F32)
    dt = jnp.exp(log_dt.astype(F32))[..., None]
    mag = jnp.exp(lr * dt)
    bar_re, bar_im = mag * jnp.cos(li * dt), mag * jnp.sin(li * dt)
    xr, xi = bar_re - 1.0, bar_im
    den = lr * lr + li * li
    fr = (xr * lr + xi * li) / den
    fi = (xi * lr - xr * li) / den
    br, bi = b_re.astype(F32)[None], b_im.astype(F32)[None]
    bb_re = fr[..., None] * br - fi[..., None] * bi
    bb_im = fr[..., None] * bi + fi[..., None] * br

    def cmul(e1, e2):
        return (e1[0] * e2[0] - e1[1] * e2[1], e1[0] * e2[1] + e1[1] * e2[0])
    rep_re = jnp.broadcast_to(bar_re, (L,) + bar_re.shape)
    rep_im = jnp.broadcast_to(bar_im, (L,) + bar_im.shape)
    cp_re, cp_im = lax.associative_scan(cmul, (rep_re, rep_im), axis=0)
    pr = jnp.concatenate([jnp.ones_like(bar_re)[None], cp_re], axis=0)
    pi = jnp.concatenate([jnp.zeros_like(bar_im)[None], cp_im], axis=0)

    ab_re = pr[..., None] * bb_re - pi[..., None] * bb_im
    ab_im = pr[..., None] * bb_im + pi[..., None] * bb_re
    cr, ci = c_re.astype(F32), c_im.astype(F32)

    kk = (jnp.einsum('gcp,tdgpe->tdgce', cr, ab_re[:L], precision=hi)
          - jnp.einsum('gcp,tdgpe->tdgce', ci, ab_im[:L], precision=hi))
    kf, kb = kk[:, 0], kk[:, 1]
    eye_c = jnp.eye(C, dtype=F32)
    diag = kf[0] + kb[0] + d.astype(F32).reshape(G, C)[:, :, None] * eye_c[None]
    kfull = jnp.concatenate([kb[1:][::-1], diag[None], kf[1:]], axis=0)
    jj = jnp.arange(L)
    idx = jj[None, :] - jj[:, None] + (L - 1)
    w5 = kfull[idx]
    w_intra = jnp.transpose(w5, (2, 4, 0, 3, 1)).reshape(G, C * L, C * L)

    ef_re = ab_re[:L, 0][::-1]
    ef_im = ab_im[:L, 0][::-1]
    eb_re = ab_re[:L, 1]
    eb_im = ab_im[:L, 1]
    w_e = jnp.concatenate([ef_re, eb_re, ef_im, eb_im], axis=2)
    w_e = jnp.transpose(w_e, (1, 3, 0, 2)).reshape(G, C * L, 4 * P)

    pf_re, pf_im = pr[1:L + 1, 0], pi[1:L + 1, 0]
    pb_re, pb_im = pr[1:L + 1, 1][::-1], pi[1:L + 1, 1][::-1]

    def cw(p_re, p_im):
        w_r = jnp.einsum('gcp,tgp->gpct', cr, p_re) - jnp.einsum('gcp,tgp->gpct', ci, p_im)
        w_i = -jnp.einsum('gcp,tgp->gpct', cr, p_im) - jnp.einsum('gcp,tgp->gpct', ci, p_re)
        return w_r, w_i
    wf_r, wf_i = cw(pf_re, pf_im)
    wb_r, wb_i = cw(pb_re, pb_im)
    w_c = jnp.concatenate([wf_r, wb_r, wf_i, wb_i], axis=1).reshape(G, 4 * P, C * L)

    rows = []
    ar = jnp.concatenate([pr[L, 0], pr[L, 1]], axis=-1)
    ai = jnp.concatenate([pi[L, 0], pi[L, 1]], axis=-1)
    step = 1
    while step < nc:
        rows += [ar, ai]
        ar, ai = ar * ar - ai * ai, 2.0 * ar * ai
        step *= 2
    if not rows:
        rows = [ar, ai]
    apow = jnp.stack(rows, axis=1)
    return w_intra.astype(BF16), w_e.astype(BF16), w_c.astype(BF16), apow


def _s5_kernel(u_ref, wi_ref, we_ref, wc_ref, ap_ref, o_ref, *, nc):
    C, P = S5_GROUP, S5_P
    u = jnp.concatenate([u_ref[c] for c in range(C)], axis=-1)
    y = jnp.dot(u, wi_ref[0], preferred_element_type=F32)
    e = jnp.dot(u, we_ref[0], preferred_element_type=F32)
    re, im = e[:, :2 * P], e[:, 2 * P:]
    R = re.shape[0]
    k = lax.broadcasted_iota(jnp.int32, (R, 2 * P), 0) % nc
    is_f = lax.broadcasted_iota(jnp.int32, (R, 2 * P), 1) < P

    def shifted(v, dist):
        prev = jnp.where(k >= dist, pltpu.roll(v, dist, 0), 0.0)
        nxt = jnp.where(k < nc - dist, pltpu.roll(v, R - dist, 0), 0.0)
        return jnp.where(is_f, prev, nxt)

    dist, i = 1, 0
    while dist < nc:
        ar = ap_ref[0, 2 * i:2 * i + 1, :]
        ai = ap_ref[0, 2 * i + 1:2 * i + 2, :]
        s_re, s_im = shifted(re, dist), shifted(im, dist)
        re, im = re + ar * s_re - ai * s_im, im + ar * s_im + ai * s_re
        dist, i = dist * 2, i + 1
    if nc > 1:
        st = jnp.concatenate([shifted(re, 1), shifted(im, 1)], axis=-1).astype(BF16)
        y = y + jnp.dot(st, wc_ref[0], preferred_element_type=F32)
    for c in range(C):
        o_ref[c] = y[:, c * S5_CHUNK:(c + 1) * S5_CHUNK].astype(o_ref.dtype)


def _s5_core(ut3, w_intra, w_e, w_c, apow, nc):
    _, R, L = ut3.shape
    C, G, P = S5_GROUP, S5_NG, S5_P
    n_ap = apow.shape[1]
    return pl.pallas_call(
        functools.partial(_s5_kernel, nc=nc),
        grid=(G,),
        in_specs=[
            pl.BlockSpec((C, R, L), lambda g: (g, 0, 0)),
            pl.BlockSpec((1, C * L, C * L), lambda g: (g, 0, 0)),
            pl.BlockSpec((1, C * L, 4 * P), lambda g: (g, 0, 0)),
            pl.BlockSpec((1, 4 * P, C * L), lambda g: (g, 0, 0)),
            pl.BlockSpec((1, n_ap, 2 * P), lambda g: (g, 0, 0)),
        ],
        out_specs=pl.BlockSpec((C, R, L), lambda g: (g, 0, 0)),
        out_shape=jax.ShapeDtypeStruct((D_S5, R, L), BF16),
        compiler_params=_params(dimension_semantics=("arbitrary",)),
        name="s5_core",
    )(ut3, w_intra, w_e, w_c, apow)


def _log_sigmoid(x):
    return jnp.minimum(x, 0.0) - jnp.log(1.0 + jnp.exp(-jnp.abs(x)))


def _mlstm_chunk(q, k, v_aug, li_row, lf_row, ct, m, tri, eye):
    L = q.shape[0]
    neg = -jnp.inf
    bcol = jnp.sum(jnp.where(tri, jnp.broadcast_to(lf_row, (L, L)), 0.0), axis=1, keepdims=True)
    brow = jnp.sum(jnp.where(eye, jnp.broadcast_to(bcol, (L, L)), 0.0), axis=0, keepdims=True)
    licol = jnp.sum(jnp.where(eye, jnp.broadcast_to(li_row, (L, L)), 0.0), axis=1, keepdims=True)
    btot = jnp.sum(lf_row, axis=1, keepdims=True)
    d = jnp.where(tri, bcol - brow + li_row, neg)
    a = bcol + m
    mt = jnp.maximum(a, jnp.max(d, axis=1, keepdims=True))
    w_intra = jnp.exp(d - mt)
    w_inter = jnp.exp(a - mt)
    s = lax.dot_general(q, k, _NT, preferred_element_type=F32) * w_intra
    r = (jnp.dot(s.astype(BF16), v_aug, preferred_element_type=F32)
         + w_inter * jnp.dot(q, ct.astype(BF16), preferred_element_type=F32))
    num, den = r[:, :ML_DH], r[:, ML_DH:ML_DH + 1]
    h = num / jnp.maximum(jnp.abs(den), jnp.exp(-mt))
    g = btot - bcol + licol
    m_new = jnp.maximum(btot + m, jnp.max(g, axis=0, keepdims=True))
    decay = jnp.exp(btot + m - m_new)
    w_state = jnp.exp(g - m_new)
    kw = (k.astype(F32) * w_state).astype(BF16)
    ct_new = decay * ct + lax.dot_general(kw, v_aug, _TN, preferred_element_type=F32)
    return h, ct_new, m_new


def _mlstm_kernel(xm_ref, op_ref, gt_ref, cw_ref, cb_ref, wq_ref, wk_ref, wv_ref, hn_ref, sk_ref, gb_ref,
                  y_ref, xc_s, q_s, k_s, v_s, hf_s, hb_s, g_s, ct_s, m_s):
    S = xm_ref.shape[0]
    L, H, DH = ML_CHUNK, ML_HEADS, ML_DH
    nc = S // L
    xm = xm_ref[...].astype(F32)
    t_idx = lax.broadcasted_iota(jnp.int32, (S, D_ML), 0)
    acc = xm * cw_ref[CONV_K // 2:CONV_K // 2 + 1, :]
    for i in range(CONV_K):
        off = i - CONV_K // 2
        if off == 0:
            continue
        sh = pltpu.roll(xm, (-off) % S, 0)
        ok = (t_idx + off >= 0) & (t_idx + off < S)
        acc = acc + jnp.where(ok, sh, 0.0) * cw_ref[i:i + 1, :]
    acc = acc + cb_ref[...]
    xc = (acc * _sigmoid(acc)).astype(BF16)
    xc_s[...] = xc
    xmb = xm_ref[...]
    for p in range(H // 2):
        sl = slice(p * 2 * DH, (p + 1) * 2 * DH)
        q_s[:, sl] = jnp.dot(xc[:, sl], wq_ref[p], preferred_element_type=F32).astype(BF16)
        k_s[:, sl] = jnp.dot(xc[:, sl], wk_ref[p], preferred_element_type=F32).astype(BF16)
        v_s[:, sl] = jnp.dot(xmb[:, sl], wv_ref[p], preferred_element_type=F32).astype(BF16)

    gp = gt_ref[0] + gb_ref[...][None]
    row = lax.broadcasted_iota(jnp.int32, gp.shape, 1)
    g_s[...] = jnp.where(row < 2 * H, gp, _log_sigmoid(gp))

    ct_s[...] = jnp.zeros_like(ct_s)
    m_s[...] = jnp.zeros_like(m_s)

    ti = lax.broadcasted_iota(jnp.int32, (L, L), 0)
    si = lax.broadcasted_iota(jnp.int32, (L, L), 1)
    tri_f, tri_b, eye = si <= ti, si >= ti, si == ti
    lane = lax.broadcasted_iota(jnp.int32, (L, DH), 1)
    ones_col = jnp.where(lane == 0, 1.0, 0.0).astype(BF16)

    def body(i, carry):
        for rev in (False, True):
            ci = (nc - 1 - i) if rev else i
            r0 = pl.multiple_of(ci * L, L)
            gch = g_s[ci]
            for h in range(H):
                cs = slice(h * DH, (h + 1) * DH)
                q = q_s[pl.ds(r0, L), cs]
                k = k_s[pl.ds(r0, L), cs]
                v_aug = jnp.concatenate([v_s[pl.ds(r0, L), cs], ones_col], axis=-1)
                gi = (1 if rev else 0) * H + h
                gf = (3 if rev else 2) * H + h
                chain = (1 if rev else 0) * H + h
                hh, ct_new, m_new = _mlstm_chunk(
                    q, k, v_aug, gch[gi:gi + 1, :], gch[gf:gf + 1, :],
                    ct_s[chain], m_s[chain], tri_b if rev else tri_f, eye)
                ct_s[chain] = ct_new
                m_s[chain] = m_new
                if rev:
                    hb_s[pl.ds(r0, L), cs] = hh
                else:
                    hf_s[pl.ds(r0, L), cs] = hh
        return carry

    lax.fori_loop(0, nc, body, 0)

    xcf = xc_s[...].astype(F32)
    o = _sigmoid(op_ref[...].astype(F32))
    for h in range(H):
        cs = slice(h * DH, (h + 1) * DH)
        hh = hf_s[:, cs] + hb_s[:, cs]
        mu = jnp.mean(hh, axis=-1, keepdims=True)
        var = jnp.mean(jnp.square(hh - mu), axis=-1, keepdims=True)
        hn = (hh - mu) * lax.rsqrt(var + EPS) * hn_ref[:, cs]
        y_ref[:, cs] = (o[:, cs] * (hn + sk_ref[:, cs] * xcf[:, cs])).astype(y_ref.dtype)


def _blockdiag2(w):
    H, DH, _ = w.shape
    z = jnp.zeros((DH, DH), w.dtype)
    out = []
    for p in range(H // 2):
        a, b = w[2 * p], w[2 * p + 1]
        out.append(jnp.concatenate([jnp.concatenate([a, z], 1), jnp.concatenate([z, b], 1)], 0))
    return jnp.stack(out)


def _mlstm(xm, op, gt4, conv_w, conv_b, wq, wk, wv, head_norm, skip, gate_bias, B, S):
    L, H, DH = ML_CHUNK, ML_HEADS, ML_DH
    nc = S // L
    wq2 = _blockdiag2(wq.astype(F32)).astype(BF16)
    wk2 = _blockdiag2(wk.astype(F32) * (DH ** -0.5)).astype(BF16)
    wv2 = _blockdiag2(wv.astype(F32)).astype(BF16)
    gb = gate_bias.astype(F32).reshape(N_GATE_COLS, 1)
    row = lambda v: v.astype(F32).reshape(1, D_ML)
    full = lambda shape: pl.BlockSpec(shape, lambda b: (0,) * len(shape))
    return pl.pallas_call(
        _mlstm_kernel,
        grid=(B,),
        in_specs=[
            pl.BlockSpec((S, D_ML), lambda b: (b, 0)),
            pl.BlockSpec((S, D_ML), lambda b: (b, 0)),
            pl.BlockSpec((1, nc, N_GATE_COLS, L), lambda b: (b, 0, 0, 0)),
            full((CONV_K, D_ML)), full((1, D_ML)),
            full((H // 2, 2 * DH, 2 * DH)), full((H // 2, 2 * DH, 2 * DH)), full((H // 2, 2 * DH, 2 * DH)),
            full((1, D_ML)), full((1, D_ML)), full((N_GATE_COLS, 1)),
        ],
        out_specs=pl.BlockSpec((S, D_ML), lambda b: (b, 0)),
        out_shape=jax.ShapeDtypeStruct((B * S, D_ML), BF16),
        scratch_shapes=[
            pltpu.VMEM((S, D_ML), BF16),
            pltpu.VMEM((S, D_ML), BF16),
            pltpu.VMEM((S, D_ML), BF16),
            pltpu.VMEM((S, D_ML), BF16),
            pltpu.VMEM((S, D_ML), F32),
            pltpu.VMEM((S, D_ML), F32),
            pltpu.VMEM((nc, N_GATE_COLS, L), F32),
            pltpu.VMEM((2 * H, DH, 2 * DH), F32),
            pltpu.VMEM((2 * H, 1, 1), F32),
        ],
        compiler_params=_params(dimension_semantics=("arbitrary",)),
        name="mlstm",
    )(xm, op, gt4, conv_w.astype(F32), row(conv_b), wq2, wk2, wv2, row(head_norm), row(skip), gb)


def _mixout_kernel(x_ref, yt_ref, yml_ref, wgt_ref, bg_ref, wo1_ref, wo2_ref, gpost_ref, gffn_ref,
                   x1_ref, h2_ref):
    g = _gelu_tanh(yt_ref[...].astype(F32))
    z = jnp.dot(wgt_ref[...], g.astype(BF16), preferred_element_type=F32) + bg_ref[...]
    s5 = (g * _sigmoid(z)).astype(BF16)
    acc = (lax.dot_general(s5, wo1_ref[...], _TN, preferred_element_type=F32)
           + jnp.dot(yml_ref[...], wo2_ref[...], preferred_element_type=F32))
    x1 = x_ref[...] + _rms(acc) * gpost_ref[...]
    x1_ref[...] = x1
    h2_ref[...] = (_rms(x1) * gffn_ref[...]).astype(BF16)


def _mixout(x2, yt, yml, w_glu, b_glu, w_out, g_post, g_ffn, tn):
    T = x2.shape[0]
    wgt = w_glu.T.astype(BF16)
    bg = b_glu.astype(F32).reshape(D_S5, 1)
    wo1 = w_out[:D_S5].astype(BF16)
    wo2 = w_out[D_S5:].astype(BF16)
    full = lambda shape: pl.BlockSpec(shape, lambda i: (0,) * len(shape))
    return pl.pallas_call(
        _mixout_kernel,
        grid=(T // tn,),
        in_specs=[
            pl.BlockSpec((tn, D_MODEL), lambda i: (i, 0)),
            pl.BlockSpec((D_S5, tn), lambda i: (0, i)),
            pl.BlockSpec((tn, D_ML), lambda i: (i, 0)),
            full((D_S5, D_S5)), full((D_S5, 1)),
            full((D_S5, D_MODEL)), full((D_ML, D_MODEL)),
            full((1, D_MODEL)), full((1, D_MODEL)),
        ],
        out_specs=[
            pl.BlockSpec((tn, D_MODEL), lambda i: (i, 0)),
            pl.BlockSpec((tn, D_MODEL), lambda i: (i, 0)),
        ],
        out_shape=[
            jax.ShapeDtypeStruct((T, D_MODEL), F32),
            jax.ShapeDtypeStruct((T, D_MODEL), BF16),
        ],
        compiler_params=_params(dimension_semantics=("arbitrary",)),
        name="mixout",
    )(x2, yt, yml, wgt, bg, wo1, wo2, g_post.astype(F32).reshape(1, D_MODEL),
      g_ffn.astype(F32).reshape(1, D_MODEL))


def _ffn_kernel(x1_ref, h2_ref, wg_ref, wu_ref, wd_ref, gn_ref, o_ref):
    h = h2_ref[...]
    acc = jnp.zeros((h.shape[0], D_MODEL), F32)
    for c in range(D_FF // FF_CHUNK):
        sl = slice(c * FF_CHUNK, (c + 1) * FF_CHUNK)
        gg = jnp.dot(h, wg_ref[:, sl], preferred_element_type=F32)
        uu = jnp.dot(h, wu_ref[:, sl], preferred_element_type=F32)
        a = (gg * _sigmoid(gg) * uu).astype(BF16)
        acc = acc + jnp.dot(a, wd_ref[sl, :], preferred_element_type=F32)
    o_ref[...] = x1_ref[...] + _rms(acc) * gn_ref[...]


def _ffn(x1, h2, w_gate, w_up, w_down, g_post, tm):
    T = x1.shape[0]
    const = lambda shape: pl.BlockSpec(shape, lambda i: (0,) * len(shape), pipeline_mode=pl.Buffered(1))
    return pl.pallas_call(
        _ffn_kernel,
        grid=(T // tm,),
        in_specs=[
            pl.BlockSpec((tm, D_MODEL), lambda i: (i, 0)),
            pl.BlockSpec((tm, D_MODEL), lambda i: (i, 0)),
            const((D_MODEL, D_FF)), const((D_MODEL, D_FF)), const((D_FF, D_MODEL)),
            const((1, D_MODEL)),
        ],
        out_specs=pl.BlockSpec((tm, D_MODEL), lambda i: (i, 0)),
        out_shape=jax.ShapeDtypeStruct((T, D_MODEL), F32),
        compiler_params=_params(dimension_semantics=("arbitrary",)),
        name="ffn",
    )(x1, h2, w_gate.astype(BF16), w_up.astype(BF16), w_down.astype(BF16),
      g_post.astype(F32).reshape(1, D_MODEL))


def _tile(T, want):
    t = min(T, want)
    assert T % t == 0
    return t


def _layer(x, norm_mix_pre, norm_mix_post, norm_ffn_pre, norm_ffn_post, w_in, ml_gate_bias,
           s5_lam_re, s5_lam_im, s5_log_dt, s5_b_re, s5_b_im, s5_c_re, s5_c_im, s5_d,
           s5_w_glu, s5_b_glu, ml_conv_w, ml_conv_b, ml_wq, ml_wk, ml_wv, ml_head_norm,
           ml_skip, w_out, w_gate, w_up, w_down):
    B, S, D = x.shape
    assert D == D_MODEL and S % S5_CHUNK == 0 and S % ML_CHUNK == 0
    T = B * S
    nc5 = S // S5_CHUNK
    assert nc5 & (nc5 - 1) == 0, "the chunk scan assumes a power-of-two chunk count"
    x2 = x.reshape(T, D)

    ut, gt, xm, op = _inproj(x2, norm_mix_pre, w_in, _tile(T, 1024))

    w_intra, w_e, w_c, apow = _s5_weights(s5_lam_re, s5_lam_im, s5_log_dt, s5_b_re, s5_b_im,
                                          s5_c_re, s5_c_im, s5_d, nc5)
    yt3 = _s5_core(ut.reshape(D_S5, T // S5_CHUNK, S5_CHUNK), w_intra, w_e, w_c, apow, nc5)
    yt = yt3.reshape(D_S5, T)

    ncm = S // ML_CHUNK
    gt4 = jnp.transpose(gt.reshape(N_GATE_COLS, B, ncm, ML_CHUNK), (1, 2, 0, 3))
    yml = _mlstm(xm, op, gt4, ml_conv_w, ml_conv_b, ml_wq, ml_wk, ml_wv, ml_head_norm, ml_skip,
                 ml_gate_bias, B, S)

    x1, h2 = _mixout(x2, yt, yml, s5_w_glu, s5_b_glu, w_out, norm_mix_post, norm_ffn_pre, _tile(T, 512))
    out = _ffn(x1, h2, w_gate, w_up, w_down, norm_ffn_post, _tile(T, 512))
    return out.reshape(B, S, D)


def kernel(x, norm_mix_pre, norm_mix_post, norm_ffn_pre, norm_ffn_post, w_in, ml_gate_bias, s5_lam_re, s5_lam_im, s5_log_dt, s5_b_re, s5_b_im, s5_c_re, s5_c_im, s5_d, s5_w_glu, s5_b_glu, ml_conv_w, ml_conv_b, ml_wq, ml_wk, ml_wv, ml_head_norm, ml_skip, w_out, w_gate, w_up, w_down):
    depth = w_in.shape[0]
    for l in range(depth):
        x = _layer(
            x, norm_mix_pre[l], norm_mix_post[l], norm_ffn_pre[l], norm_ffn_post[l], w_in[l],
            ml_gate_bias[l], s5_lam_re[l], s5_lam_im[l], s5_log_dt[l], s5_b_re[l], s5_b_im[l],
            s5_c_re[l], s5_c_im[l], s5_d[l], s5_w_glu[l], s5_b_glu[l], ml_conv_w[l], ml_conv_b[l],
            ml_wq[l], ml_wk[l], ml_wv[l], ml_head_norm[l], ml_skip[l], w_out[l], w_gate[l],
            w_up[l], w_down[l])
    return x
```

```python
import functools
import math

import jax
import jax.numpy as jnp
from jax import lax
from jax.experimental import pallas as pl
from jax.experimental.pallas import tpu as pltpu

D_MODEL = 1024
D_S5 = 512
D_ML = 512
S5_GROUP = 16
S5_NG = D_S5 // S5_GROUP
S5_P = 64
ML_HEADS = 4
ML_DH = D_ML // ML_HEADS
CONV_K = 5
N_GATES = 4
N_GATE_COLS = N_GATES * ML_HEADS
D_FF = 256 * math.ceil(8 * D_MODEL / 3 / 256)
EPS = 1e-6

LANES = 128
S5_CHUNK = LANES
ML_CHUNK = 128
FF_CHUNK = 256
VMEM_LIMIT = 56 * 1024 * 1024

F32 = jnp.float32
BF16 = jnp.bfloat16

_NT = (((1,), (1,)), ((), ()))
_TN = (((0,), (0,)), ((), ()))


def _rms(x):
    return x * lax.rsqrt(jnp.mean(x * x, axis=-1, keepdims=True) + EPS)


def _sigmoid(x):
    return 1.0 / (1.0 + jnp.exp(-x))


def _gelu_tanh(x):
    c = math.sqrt(2.0 / math.pi)
    return 0.5 * x * (1.0 + jnp.tanh(c * (x + 0.044715 * (x * x * x))))


def _params(**kw):
    return pltpu.CompilerParams(vmem_limit_bytes=VMEM_LIMIT, **kw)


def _inproj_kernel(x_ref, g_ref, wt_ref, w_ref, ut_ref, gt_ref, xm_ref, op_ref):
    x = x_ref[...]
    h = (_rms(x) * g_ref[...]).astype(BF16)
    st = lax.dot_general(wt_ref[...], h, _NT, preferred_element_type=F32)
    ut_ref[...] = st[:D_S5].astype(BF16)
    gt_ref[...] = st[D_S5:]
    p = jnp.dot(h, w_ref[...], preferred_element_type=F32)
    xm_ref[...] = p[:, :D_ML].astype(BF16)
    op_ref[...] = p[:, D_ML:].astype(BF16)


def _inproj(x2, g_pre, w_in, tm):
    T = x2.shape[0]
    wt = jnp.concatenate([w_in[:, :D_S5], w_in[:, D_S5 + 2 * D_ML:]], axis=1).T.astype(BF16)
    w = w_in[:, D_S5:D_S5 + 2 * D_ML].astype(BF16)
    n_t = D_S5 + N_GATE_COLS
    return pl.pallas_call(
        _inproj_kernel,
        grid=(T // tm,),
        in_specs=[
            pl.BlockSpec((tm, D_MODEL), lambda i: (i, 0)),
            pl.BlockSpec((1, D_MODEL), lambda i: (0, 0)),
            pl.BlockSpec((n_t, D_MODEL), lambda i: (0, 0)),
            pl.BlockSpec((D_MODEL, 2 * D_ML), lambda i: (0, 0)),
        ],
        out_specs=[
            pl.BlockSpec((D_S5, tm), lambda i: (0, i)),
            pl.BlockSpec((N_GATE_COLS, tm), lambda i: (0, i)),
            pl.BlockSpec((tm, D_ML), lambda i: (i, 0)),
            pl.BlockSpec((tm, D_ML), lambda i: (i, 0)),
        ],
        out_shape=[
            jax.ShapeDtypeStruct((D_S5, T), BF16),
            jax.ShapeDtypeStruct((N_GATE_COLS, T), F32),
            jax.ShapeDtypeStruct((T, D_ML), BF16),
            jax.ShapeDtypeStruct((T, D_ML), BF16),
        ],
        compiler_params=_params(dimension_semantics=("arbitrary",)),
        name="inproj",
    )(x2, g_pre.reshape(1, D_MODEL), wt, w)


def _s5_weights(lam_re, lam_im, log_dt, b_re, b_im, c_re, c_im, d, nc):
    L, G, P, C = S5_CHUNK, S5_NG, S5_P, S5_GROUP
    hi = lax.Precision.HIGHEST
    lr, li = lam_re.astype(F32), lam_im.astype(F32)
    dt = jnp.exp(log_dt.astype(F32))[..., None]
    mag = jnp.exp(lr * dt)
    bar_re, bar_im = mag * jnp.cos(li * dt), mag * jnp.sin(li * dt)
    xr, xi = bar_re - 1.0, bar_im
    den = lr * lr + li * li
    fr = (xr * lr + xi * li) / den
    fi = (xi * lr - xr * li) / den
    br = jnp.transpose(b_re.astype(F32), (0, 2, 1))[None]
    bi = jnp.transpose(b_im.astype(F32), (0, 2, 1))[None]
    bb_re = fr[:, :, None, :] * br - fi[:, :, None, :] * bi
    bb_im = fr[:, :, None, :] * bi + fi[:, :, None, :] * br

    def cmul(e1, e2):
        return (e1[0] * e2[0] - e1[1] * e2[1], e1[0] * e2[1] + e1[1] * e2[0])
    rep_re = jnp.broadcast_to(bar_re, (L,) + bar_re.shape)
    rep_im = jnp.broadcast_to(bar_im, (L,) + bar_im.shape)
    cp_re, cp_im = lax.associative_scan(cmul, (rep_re, rep_im), axis=0)
    pr = jnp.concatenate([jnp.ones_like(bar_re)[None], cp_re], axis=0)
    pi = jnp.concatenate([jnp.zeros_like(bar_im)[None], cp_im], axis=0)
    cr, ci = c_re.astype(F32), c_im.astype(F32)

    x_re = cr[None, :, None] * bb_re[:, :, :, None] - ci[None, :, None] * bb_im[:, :, :, None]
    x_im = -cr[None, :, None] * bb_im[:, :, :, None] - ci[None, :, None] * bb_re[:, :, :, None]
    prt = jnp.transpose(pr[:L], (1, 2, 3, 0))
    pit = jnp.transpose(pi[:L], (1, 2, 3, 0))
    kk = (jnp.einsum('dgecp,dgpt->dgect', x_re, prt, precision=hi)
          + jnp.einsum('dgecp,dgpt->dgect', x_im, pit, precision=hi))
    kf, kb = kk[0], kk[1]
    eye_c = jnp.eye(C, dtype=F32)
    diag = kf[..., 0] + kb[..., 0] + d.astype(F32).reshape(G, 1, C) * eye_c[None]
    kfull = jnp.concatenate([kb[..., 1:][..., ::-1], diag[..., None], kf[..., 1:],
                             jnp.zeros((G, C, C, 1), F32)], axis=-1).reshape(G, C * C, 2 * L)

    def ew(p_re, p_im, b_r, b_i):
        a_r = jnp.transpose(p_re, (1, 0, 2))[:, None]
        a_i = jnp.transpose(p_im, (1, 0, 2))[:, None]
        return a_r * b_r[:, :, None] - a_i * b_i[:, :, None], a_r * b_i[:, :, None] + a_i * b_r[:, :, None]
    ef_r, ef_i = ew(pr[:L, 0][::-1], pi[:L, 0][::-1], bb_re[0], bb_im[0])
    eb_r, eb_i = ew(pr[:L, 1], pi[:L, 1], bb_re[1], bb_im[1])
    w_e = jnp.concatenate([ef_r, eb_r, ef_i, eb_i], axis=-1).reshape(G, C * L, 4 * P)

    def cw(p_re, p_im):
        q_r = jnp.transpose(p_re, (1, 2, 0))[:, :, None]
        q_i = jnp.transpose(p_im, (1, 2, 0))[:, :, None]
        c_r = jnp.transpose(cr, (0, 2, 1))[..., None]
        c_i = jnp.transpose(ci, (0, 2, 1))[..., None]
        return c_r * q_r - c_i * q_i, -c_r * q_i - c_i * q_r
    wf_r, wf_i = cw(pr[1:L + 1, 0], pi[1:L + 1, 0])
    wb_r, wb_i = cw(pr[1:L + 1, 1][::-1], pi[1:L + 1, 1][::-1])
    w_c = jnp.concatenate([wf_r, wb_r, wf_i, wb_i], axis=1).reshape(G, 4 * P, C * L)

    rows = []
    ar = jnp.concatenate([pr[L, 0], pr[L, 1]], axis=-1)
    ai = jnp.concatenate([pi[L, 0], pi[L, 1]], axis=-1)
    step = 1
    while step < nc:
        rows += [ar, ai]
        ar, ai = ar * ar - ai * ai, 2.0 * ar * ai
        step *= 2
    if not rows:
        rows = [ar, ai]
    apow = jnp.stack(rows, axis=1)
    return kfull, w_e.astype(BF16), w_c.astype(BF16), apow


def _s5_kernel(u_ref, kf_ref, we_ref, wc_ref, ap_ref, o_ref, w_s, *, nc):
    C, P, L = S5_GROUP, S5_P, S5_CHUNK

    def build(cp, carry):
        r0 = pl.multiple_of(cp * L, L)
        kv = kf_ref[0, pl.ds(pl.multiple_of(cp * C, C), C), :]
        for c in range(C):
            vb = jnp.broadcast_to(kv[c:c + 1, :], (L, 2 * L))
            blk = pltpu.roll(vb, L + 1, 1, stride=1, stride_axis=0)[:, :L]
            w_s[pl.ds(r0, L), c * L:(c + 1) * L] = blk.astype(BF16)
        return carry
    lax.fori_loop(0, C, build, 0)

    u = jnp.concatenate([u_ref[c] for c in range(C)], axis=-1)
    y = jnp.dot(u, w_s[...], preferred_element_type=F32)
    e = jnp.dot(u, we_ref[0], preferred_element_type=F32)
    re, im = e[:, :2 * P], e[:, 2 * P:]
    R = re.shape[0]
    k = lax.broadcasted_iota(jnp.int32, (R, 2 * P), 0) % nc
    is_f = lax.broadcasted_iota(jnp.int32, (R, 2 * P), 1) < P

    def shifted(v, dist):
        prev = jnp.where(k >= dist, pltpu.roll(v, dist, 0), 0.0)
        nxt = jnp.where(k < nc - dist, pltpu.roll(v, R - dist, 0), 0.0)
        return jnp.where(is_f, prev, nxt)

    dist, i = 1, 0
    while dist < nc:
        ar = ap_ref[0, 2 * i:2 * i + 1, :]
        ai = ap_ref[0, 2 * i + 1:2 * i + 2, :]
        s_re, s_im = shifted(re, dist), shifted(im, dist)
        re, im = re + ar * s_re - ai * s_im, im + ar * s_im + ai * s_re
        dist, i = dist * 2, i + 1
    if nc > 1:
        st = jnp.concatenate([shifted(re, 1), shifted(im, 1)], axis=-1).astype(BF16)
        y = y + jnp.dot(st, wc_ref[0], preferred_element_type=F32)
    for c in range(C):
        o_ref[c] = y[:, c * L:(c + 1) * L].astype(o_ref.dtype)


def _s5_core(ut3, kfull, w_e, w_c, apow, nc):
    _, R, L = ut3.shape
    C, G, P = S5_GROUP, S5_NG, S5_P
    n_ap = apow.shape[1]
    return pl.pallas_call(
        functools.partial(_s5_kernel, nc=nc),
        grid=(G,),
        in_specs=[
            pl.BlockSpec((C, R, L), lambda g: (g, 0, 0)),
            pl.BlockSpec((1, C * C, 2 * L), lambda g: (g, 0, 0)),
            pl.BlockSpec((1, C * L, 4 * P), lambda g: (g, 0, 0)),
            pl.BlockSpec((1, 4 * P, C * L), lambda g: (g, 0, 0)),
            pl.BlockSpec((1, n_ap, 2 * P), lambda g: (g, 0, 0)),
        ],
        out_specs=pl.BlockSpec((C, R, L), lambda g: (g, 0, 0)),
        out_shape=jax.ShapeDtypeStruct((D_S5, R, L), BF16),
        scratch_shapes=[pltpu.VMEM((C * L, C * L), BF16)],
        compiler_params=_params(dimension_semantics=("arbitrary",)),
        name="s5_core",
    )(ut3, kfull, w_e, w_c, apow)


def _log_sigmoid(x):
    return jnp.minimum(x, 0.0) - jnp.log(1.0 + jnp.exp(-jnp.abs(x)))


def _mlstm_chunk(q, k, v_aug, li_row, lf_row, ct, m, tri, eye):
    L = q.shape[0]
    neg = -jnp.inf
    bcol = jnp.sum(jnp.where(tri, jnp.broadcast_to(lf_row, (L, L)), 0.0), axis=1, keepdims=True)
    brow = jnp.sum(jnp.where(eye, jnp.broadcast_to(bcol, (L, L)), 0.0), axis=0, keepdims=True)
    licol = jnp.sum(jnp.where(eye, jnp.broadcast_to(li_row, (L, L)), 0.0), axis=1, keepdims=True)
    btot = jnp.sum(lf_row, axis=1, keepdims=True)
    d = jnp.where(tri, bcol - brow + li_row, neg)
    a = bcol + m
    mt = jnp.maximum(a, jnp.max(d, axis=1, keepdims=True))
    w_intra = jnp.exp(d - mt)
    w_inter = jnp.exp(a - mt)
    s = lax.dot_general(q, k, _NT, preferred_element_type=F32) * w_intra
    r = (jnp.dot(s.astype(BF16), v_aug, preferred_element_type=F32)
         + w_inter * jnp.dot(q, ct.astype(BF16), preferred_element_type=F32))
    num, den = r[:, :ML_DH], r[:, ML_DH:ML_DH + 1]
    h = num / jnp.maximum(jnp.abs(den), jnp.exp(-mt))
    g = btot - bcol + licol
    m_new = jnp.maximum(btot + m, jnp.max(g, axis=0, keepdims=True))
    decay = jnp.exp(btot + m - m_new)
    w_state = jnp.exp(g - m_new)
    kw = (k.astype(F32) * w_state).astype(BF16)
    ct_new = decay * ct + lax.dot_general(kw, v_aug, _TN, preferred_element_type=F32)
    return h, ct_new, m_new


def _mlstm_kernel(xm_ref, op_ref, gt_ref, cw_ref, cb_ref, wq_ref, wk_ref, wv_ref, hn_ref, sk_ref, gb_ref,
                  y_ref, xc_s, q_s, k_s, v_s, hf_s, hb_s, g_s, ct_s, m_s):
    S = xm_ref.shape[0]
    L, H, DH = ML_CHUNK, ML_HEADS, ML_DH
    nc = S // L
    xm = xm_ref[...].astype(F32)
    t_idx = lax.broadcasted_iota(jnp.int32, (S, D_ML), 0)
    acc = xm * cw_ref[CONV_K // 2:CONV_K // 2 + 1, :]
    for i in range(CONV_K):
        off = i - CONV_K // 2
        if off == 0:
            continue
        sh = pltpu.roll(xm, (-off) % S, 0)
        ok = (t_idx + off >= 0) & (t_idx + off < S)
        acc = acc + jnp.where(ok, sh, 0.0) * cw_ref[i:i + 1, :]
    acc = acc + cb_ref[...]
    xc = (acc * _sigmoid(acc)).astype(BF16)
    xc_s[...] = xc
    xmb = xm_ref[...]
    for p in range(H // 2):
        sl = slice(p * 2 * DH, (p + 1) * 2 * DH)
        q_s[:, sl] = jnp.dot(xc[:, sl], wq_ref[p], preferred_element_type=F32).astype(BF16)
        k_s[:, sl] = jnp.dot(xc[:, sl], wk_ref[p], preferred_element_type=F32).astype(BF16)
        v_s[:, sl] = jnp.dot(xmb[:, sl], wv_ref[p], preferred_element_type=F32).astype(BF16)

    gp = gt_ref[0] + gb_ref[...][None]
    row = lax.broadcasted_iota(jnp.int32, gp.shape, 1)
    g_s[...] = jnp.where(row < 2 * H, gp, _log_sigmoid(gp))

    ct_s[...] = jnp.zeros_like(ct_s)
    m_s[...] = jnp.zeros_like(m_s)

    ti = lax.broadcasted_iota(jnp.int32, (L, L), 0)
    si = lax.broadcasted_iota(jnp.int32, (L, L), 1)
    tri_f, tri_b, eye = si <= ti, si >= ti, si == ti
    lane = lax.broadcasted_iota(jnp.int32, (L, DH), 1)
    ones_col = jnp.where(lane == 0, 1.0, 0.0).astype(BF16)

    def body(i, carry):
        for rev in (False, True):
            ci = (nc - 1 - i) if rev else i
            r0 = pl.multiple_of(ci * L, L)
            gch = g_s[ci]
            for h in range(H):
                cs = slice(h * DH, (h + 1) * DH)
                q = q_s[pl.ds(r0, L), cs]
                k = k_s[pl.ds(r0, L), cs]
                v_aug = jnp.concatenate([v_s[pl.ds(r0, L), cs], ones_col], axis=-1)
                gi = (1 if rev else 0) * H + h
                gf = (3 if rev else 2) * H + h
                chain = (1 if rev else 0) * H + h
                hh, ct_new, m_new = _mlstm_chunk(
                    q, k, v_aug, gch[gi:gi + 1, :], gch[gf:gf + 1, :],
                    ct_s[chain], m_s[chain], tri_b if rev else tri_f, eye)
                ct_s[chain] = ct_new
                m_s[chain] = m_new
                if rev:
                    hb_s[pl.ds(r0, L), cs] = hh
                else:
                    hf_s[pl.ds(r0, L), cs] = hh
        return carry

    lax.fori_loop(0, nc, body, 0)

    xcf = xc_s[...].astype(F32)
    o = _sigmoid(op_ref[...].astype(F32))
    for h in range(H):
        cs = slice(h * DH, (h + 1) * DH)
        hh = hf_s[:, cs] + hb_s[:, cs]
        mu = jnp.mean(hh, axis=-1, keepdims=True)
        var = jnp.mean(jnp.square(hh - mu), axis=-1, keepdims=True)
        hn = (hh - mu) * lax.rsqrt(var + EPS) * hn_ref[:, cs]
        y_ref[:, cs] = (o[:, cs] * (hn + sk_ref[:, cs] * xcf[:, cs])).astype(y_ref.dtype)


def _blockdiag2(w):
    H, DH, _ = w.shape
    z = jnp.zeros((DH, DH), w.dtype)
    out = []
    for p in range(H // 2):
        a, b = w[2 * p], w[2 * p + 1]
        out.append(jnp.concatenate([jnp.concatenate([a, z], 1), jnp.concatenate([z, b], 1)], 0))
    return jnp.stack(out)


def _mlstm(xm, op, gt4, conv_w, conv_b, wq, wk, wv, head_norm, skip, gate_bias, B, S):
    L, H, DH = ML_CHUNK, ML_HEADS, ML_DH
    nc = S // L
    wq2 = _blockdiag2(wq.astype(F32)).astype(BF16)
    wk2 = _blockdiag2(wk.astype(F32) * (DH ** -0.5)).astype(BF16)
    wv2 = _blockdiag2(wv.astype(F32)).astype(BF16)
    gb = gate_bias.astype(F32).reshape(N_GATE_COLS, 1)
    row = lambda v: v.astype(F32).reshape(1, D_ML)
    full = lambda shape: pl.BlockSpec(shape, lambda b: (0,) * len(shape))
    return pl.pallas_call(
        _mlstm_kernel,
        grid=(B,),
        in_specs=[
            pl.BlockSpec((S, D_ML), lambda b: (b, 0)),
            pl.BlockSpec((S, D_ML), lambda b: (b, 0)),
            pl.BlockSpec((1, nc, N_GATE_COLS, L), lambda b: (b, 0, 0, 0)),
            full((CONV_K, D_ML)), full((1, D_ML)),
            full((H // 2, 2 * DH, 2 * DH)), full((H // 2, 2 * DH, 2 * DH)), full((H // 2, 2 * DH, 2 * DH)),
            full((1, D_ML)), full((1, D_ML)), full((N_GATE_COLS, 1)),
        ],
        out_specs=pl.BlockSpec((S, D_ML), lambda b: (b, 0)),
        out_shape=jax.ShapeDtypeStruct((B * S, D_ML), BF16),
        scratch_shapes=[
            pltpu.VMEM((S, D_ML), BF16),
            pltpu.VMEM((S, D_ML), BF16),
            pltpu.VMEM((S, D_ML), BF16),
            pltpu.VMEM((S, D_ML), BF16),
            pltpu.VMEM((S, D_ML), F32),
            pltpu.VMEM((S, D_ML), F32),
            pltpu.VMEM((nc, N_GATE_COLS, L), F32),
            pltpu.VMEM((2 * H, DH, 2 * DH), F32),
            pltpu.VMEM((2 * H, 1, 1), F32),
        ],
        compiler_params=_params(dimension_semantics=("arbitrary",)),
        name="mlstm",
    )(xm, op, gt4, conv_w.astype(F32), row(conv_b), wq2, wk2, wv2, row(head_norm), row(skip), gb)


def _mixout_kernel(x_ref, yt_ref, yml_ref, wgt_ref, bg_ref, wo1_ref, wo2_ref, gpost_ref, gffn_ref,
                   x1_ref, h2_ref):
    g = _gelu_tanh(yt_ref[...].astype(F32))
    z = jnp.dot(wgt_ref[...], g.astype(BF16), preferred_element_type=F32) + bg_ref[...]
    s5 = (g * _sigmoid(z)).astype(BF16)
    acc = (lax.dot_general(s5, wo1_ref[...], _TN, preferred_element_type=F32)
           + jnp.dot(yml_ref[...], wo2_ref[...], preferred_element_type=F32))
    x1 = x_ref[...] + _rms(acc) * gpost_ref[...]
    x1_ref[...] = x1
    h2_ref[...] = (_rms(x1) * gffn_ref[...]).astype(BF16)


def _mixout(x2, yt, yml, w_glu, b_glu, w_out, g_post, g_ffn, tn):
    T = x2.shape[0]
    wgt = w_glu.T.astype(BF16)
    bg = b_glu.astype(F32).reshape(D_S5, 1)
    wo1 = w_out[:D_S5].astype(BF16)
    wo2 = w_out[D_S5:].astype(BF16)
    full = lambda shape: pl.BlockSpec(shape, lambda i: (0,) * len(shape))
    return pl.pallas_call(
        _mixout_kernel,
        grid=(T // tn,),
        in_specs=[
            pl.BlockSpec((tn, D_MODEL), lambda i: (i, 0)),
            pl.BlockSpec((D_S5, tn), lambda i: (0, i)),
            pl.BlockSpec((tn, D_ML), lambda i: (i, 0)),
            full((D_S5, D_S5)), full((D_S5, 1)),
            full((D_S5, D_MODEL)), full((D_ML, D_MODEL)),
            full((1, D_MODEL)), full((1, D_MODEL)),
        ],
        out_specs=[
            pl.BlockSpec((tn, D_MODEL), lambda i: (i, 0)),
            pl.BlockSpec((tn, D_MODEL), lambda i: (i, 0)),
        ],
        out_shape=[
            jax.ShapeDtypeStruct((T, D_MODEL), F32),
            jax.ShapeDtypeStruct((T, D_MODEL), BF16),
        ],
        compiler_params=_params(dimension_semantics=("arbitrary",)),
        name="mixout",
    )(x2, yt, yml, wgt, bg, wo1, wo2, g_post.astype(F32).reshape(1, D_MODEL),
      g_ffn.astype(F32).reshape(1, D_MODEL))


def _ffn_kernel(x1_ref, h2_ref, wg_ref, wu_ref, wd_ref, gn_ref, o_ref):
    h = h2_ref[...]
    acc = jnp.zeros((h.shape[0], D_MODEL), F32)
    for c in range(D_FF // FF_CHUNK):
        sl = slice(c * FF_CHUNK, (c + 1) * FF_CHUNK)
        gg = jnp.dot(h, wg_ref[:, sl], preferred_element_type=F32)
        uu = jnp.dot(h, wu_ref[:, sl], preferred_element_type=F32)
        a = (gg * _sigmoid(gg) * uu).astype(BF16)
        acc = acc + jnp.dot(a, wd_ref[sl, :], preferred_element_type=F32)
    o_ref[...] = x1_ref[...] + _rms(acc) * gn_ref[...]


def _ffn(x1, h2, w_gate, w_up, w_down, g_post, tm):
    T = x1.shape[0]
    const = lambda shape: pl.BlockSpec(shape, lambda i: (0,) * len(shape), pipeline_mode=pl.Buffered(1))
    return pl.pallas_call(
        _ffn_kernel,
        grid=(T // tm,),
        in_specs=[
            pl.BlockSpec((tm, D_MODEL), lambda i: (i, 0)),
            pl.BlockSpec((tm, D_MODEL), lambda i: (i, 0)),
            const((D_MODEL, D_FF)), const((D_MODEL, D_FF)), const((D_FF, D_MODEL)),
            const((1, D_MODEL)),
        ],
        out_specs=pl.BlockSpec((tm, D_MODEL), lambda i: (i, 0)),
        out_shape=jax.ShapeDtypeStruct((T, D_MODEL), F32),
        compiler_params=_params(dimension_semantics=("arbitrary",)),
        name="ffn",
    )(x1, h2, w_gate.astype(BF16), w_up.astype(BF16), w_down.astype(BF16),
      g_post.astype(F32).reshape(1, D_MODEL))


def _tile(T, want):
    t = min(T, want)
    assert T % t == 0
    return t


def _layer(x, norm_mix_pre, norm_mix_post, norm_ffn_pre, norm_ffn_post, w_in, ml_gate_bias,
           s5_lam_re, s5_lam_im, s5_log_dt, s5_b_re, s5_b_im, s5_c_re, s5_c_im, s5_d,
           s5_w_glu, s5_b_glu, ml_conv_w, ml_conv_b, ml_wq, ml_wk, ml_wv, ml_head_norm,
           ml_skip, w_out, w_gate, w_up, w_down):
    B, S, D = x.shape
    assert D == D_MODEL and S % S5_CHUNK == 0 and S % ML_CHUNK == 0
    T = B * S
    nc5 = S // S5_CHUNK
    assert nc5 & (nc5 - 1) == 0, "the chunk scan assumes a power-of-two chunk count"
    x2 = x.reshape(T, D)

    ut, gt, xm, op = _inproj(x2, norm_mix_pre, w_in, _tile(T, 1024))

    kfull, w_e, w_c, apow = _s5_weights(s5_lam_re, s5_lam_im, s5_log_dt, s5_b_re, s5_b_im,
                                        s5_c_re, s5_c_im, s5_d, nc5)
    yt3 = _s5_core(ut.reshape(D_S5, T // S5_CHUNK, S5_CHUNK), kfull, w_e, w_c, apow, nc5)
    yt = yt3.reshape(D_S5, T)

    ncm = S // ML_CHUNK
    gt4 = jnp.transpose(gt.reshape(N_GATE_COLS, B, ncm, ML_CHUNK), (1, 2, 0, 3))
    yml = _mlstm(xm, op, gt4, ml_conv_w, ml_conv_b, ml_wq, ml_wk, ml_wv, ml_head_norm, ml_skip,
                 ml_gate_bias, B, S)

    x1, h2 = _mixout(x2, yt, yml, s5_w_glu, s5_b_glu, w_out, norm_mix_post, norm_ffn_pre, _tile(T, 512))
    out = _ffn(x1, h2, w_gate, w_up, w_down, norm_ffn_post, _tile(T, 512))
    return out.reshape(B, S, D)


def kernel(x, norm_mix_pre, norm_mix_post, norm_ffn_pre, norm_ffn_post, w_in, ml_gate_bias, s5_lam_re, s5_lam_im, s5_log_dt, s5_b_re, s5_b_im, s5_c_re, s5_c_im, s5_d, s5_w_glu, s5_b_glu, ml_conv_w, ml_conv_b, ml_wq, ml_wk, ml_wv, ml_head_norm, ml_skip, w_out, w_gate, w_up, w_down):
    depth = w_in.shape[0]
    for l in range(depth):
        x = _layer(
            x, norm_mix_pre[l], norm_mix_post[l], norm_ffn_pre[l], norm_ffn_post[l], w_in[l],
            ml_gate_bias[l], s5_lam_re[l], s5_lam_im[l], s5_log_dt[l], s5_b_re[l], s5_b_im[l],
            s5_c_re[l], s5_c_im[l], s5_d[l], s5_w_glu[l], s5_b_glu[l], ml_conv_w[l], ml_conv_b[l],
            ml_wq[l], ml_wk[l], ml_wv[l], ml_head_norm[l], ml_skip[l], w_out[l], w_gate[l],
            w_up[l], w_down[l])
    return x
```

```python
import functools
import math

import numpy as np
import jax
import jax.numpy as jnp
from jax import lax
from jax.experimental import pallas as pl
from jax.experimental.pallas import tpu as pltpu

D_MODEL = 1024
D_S5 = 512
D_ML = 512
S5_GROUP = 16
S5_NG = D_S5 // S5_GROUP
S5_P = 64
ML_HEADS = 4
ML_DH = D_ML // ML_HEADS
CONV_K = 5
N_GATES = 4
N_GATE_COLS = N_GATES * ML_HEADS
D_FF = 256 * math.ceil(8 * D_MODEL / 3 / 256)
EPS = 1e-6

LANES = 128
SUBLANES = 8
S5_CHUNK = LANES
ML_CHUNK = LANES
FF_CHUNK = 256
VMEM_LIMIT = 56 * 1024 * 1024

F32 = jnp.float32
BF16 = jnp.bfloat16

_NT = (((1,), (1,)), ((), ()))


def _rms(x):
    return x * lax.rsqrt(jnp.mean(x * x, axis=-1, keepdims=True) + EPS)


def _sigmoid(x):
    return 0.5 * jnp.tanh(0.5 * x) + 0.5


def _gelu_tanh(x):
    c = math.sqrt(2.0 / math.pi)
    return 0.5 * x * (1.0 + jnp.tanh(c * (x + 0.044715 * (x * x * x))))


def _params(**kw):
    return pltpu.CompilerParams(vmem_limit_bytes=VMEM_LIMIT, **kw)


def _inproj_kernel(x_ref, g_ref, wt_ref, w_ref, ut_ref, gt_ref, xm_ref, op_ref):
    x = x_ref[...]
    h = (_rms(x) * g_ref[...]).astype(BF16)
    st = lax.dot_general(wt_ref[...], h, _NT, preferred_element_type=F32)
    ut_ref[...] = st[:D_S5].astype(BF16)
    gt_ref[...] = st[D_S5:]
    p = jnp.dot(h, w_ref[...], preferred_element_type=F32)
    xm_ref[...] = p[:, :D_ML].astype(BF16)
    op_ref[...] = p[:, D_ML:].astype(BF16)


def _inproj(x2, g_pre, w_in, tm):
    T = x2.shape[0]
    wt = jnp.concatenate([w_in[:, :D_S5], w_in[:, D_S5 + 2 * D_ML:]], axis=1).T.astype(BF16)
    w = w_in[:, D_S5:D_S5 + 2 * D_ML].astype(BF16)
    n_t = D_S5 + N_GATE_COLS
    return pl.pallas_call(
        _inproj_kernel,
        grid=(T // tm,),
        in_specs=[
            pl.BlockSpec((tm, D_MODEL), lambda i: (i, 0)),
            pl.BlockSpec((1, D_MODEL), lambda i: (0, 0)),
            pl.BlockSpec((n_t, D_MODEL), lambda i: (0, 0)),
            pl.BlockSpec((D_MODEL, 2 * D_ML), lambda i: (0, 0)),
        ],
        out_specs=[
            pl.BlockSpec((D_S5, tm), lambda i: (0, i)),
            pl.BlockSpec((N_GATE_COLS, tm), lambda i: (0, i)),
            pl.BlockSpec((tm, D_ML), lambda i: (i, 0)),
            pl.BlockSpec((tm, D_ML), lambda i: (i, 0)),
        ],
        out_shape=[
            jax.ShapeDtypeStruct((D_S5, T), BF16),
            jax.ShapeDtypeStruct((N_GATE_COLS, T), F32),
            jax.ShapeDtypeStruct((T, D_ML), BF16),
            jax.ShapeDtypeStruct((T, D_ML), BF16),
        ],
        compiler_params=_params(dimension_semantics=("arbitrary",)),
        name="inproj",
    )(x2, g_pre.reshape(1, D_MODEL), wt, w)


def _s5_weights(lam_re, lam_im, log_dt, b_re, b_im, c_re, c_im, d, nc):
    L, G, P, C = S5_CHUNK, S5_NG, S5_P, S5_GROUP
    hi = lax.Precision.HIGHEST
    lr, li = lam_re.astype(F32), lam_im.astype(F32)
    dt = jnp.exp(log_dt.astype(F32))[..., None]
    mag = jnp.exp(lr * dt)
    bar_re, bar_im = mag * jnp.cos(li * dt), mag * jnp.sin(li * dt)
    xr, xi = bar_re - 1.0, bar_im
    den = lr * lr + li * li
    fr = (xr * lr + xi * li) / den
    fi = (xi * lr - xr * li) / den
    br = jnp.transpose(b_re.astype(F32), (0, 2, 1))[None]
    bi = jnp.transpose(b_im.astype(F32), (0, 2, 1))[None]
    bb_re = fr[:, :, None, :] * br - fi[:, :, None, :] * bi
    bb_im = fr[:, :, None, :] * bi + fi[:, :, None, :] * br
    cr, ci = c_re.astype(F32), c_im.astype(F32)

    sq = [(bar_re, bar_im)]
    for _ in range(L.bit_length() - 1):
        r, i = sq[-1]
        sq.append((r * r - i * i, 2.0 * r * i))

    def power(tau, direction, valid=None):
        tau = np.asarray(tau)
        re = jnp.ones((G, P, tau.size), F32)
        im = jnp.zeros((G, P, tau.size), F32)
        for bit, (s_re, s_im) in enumerate(sq):
            sel = jnp.asarray(((tau >> bit) & 1).astype(bool))
            f_re = jnp.where(sel, s_re[direction][..., None], 1.0)
            f_im = jnp.where(sel, s_im[direction][..., None], 0.0)
            re, im = re * f_re - im * f_im, re * f_im + im * f_re
        if valid is not None:
            ok = jnp.asarray(np.asarray(valid))
            re, im = jnp.where(ok, re, 0.0), jnp.where(ok, im, 0.0)
        return re, im

    m = np.arange(2 * L)
    kf_re, kf_im = power(np.maximum(m - (L - 1), 0), 0, (m >= L - 1) & (m <= 2 * L - 2))
    kb_re, kb_im = power(np.maximum((L - 1) - m, 0), 1, m <= L - 1)
    tab = jnp.stack([jnp.concatenate([kf_re, kf_im], axis=1),
                     jnp.concatenate([kb_re, kb_im], axis=1)])
    x_re = cr[None, :, None] * bb_re[:, :, :, None] - ci[None, :, None] * bb_im[:, :, :, None]
    x_im = -cr[None, :, None] * bb_im[:, :, :, None] - ci[None, :, None] * bb_re[:, :, :, None]
    xx = jnp.concatenate([x_re, x_im], axis=-1)
    kfull = jnp.einsum('dgecp,dgpm->gecm', xx, tab, precision=hi)
    at_zero = jnp.asarray(m == L - 1)
    skip = d.astype(F32).reshape(G, 1, C, 1) * jnp.eye(C, dtype=F32)[None, :, :, None]
    kfull = (kfull + jnp.where(at_zero, skip, 0.0)).reshape(G, C * C, 2 * L)

    j = np.arange(L)
    ef_re, ef_im = power(L - 1 - j, 0)
    eb_re, eb_im = power(j, 1)
    a_re = jnp.transpose(jnp.concatenate([ef_re, eb_re], axis=1), (0, 2, 1))
    a_im = jnp.transpose(jnp.concatenate([ef_im, eb_im], axis=1), (0, 2, 1))
    b2_re = jnp.concatenate([bb_re[0], bb_re[1]], axis=-1)
    b2_im = jnp.concatenate([bb_im[0], bb_im[1]], axis=-1)
    a1 = jnp.concatenate([a_re, a_re], axis=-1)[:, None]
    a2 = jnp.concatenate([-a_im, a_im], axis=-1)[:, None]
    b1 = jnp.concatenate([b2_re, b2_im], axis=-1)[:, :, None]
    b2 = jnp.concatenate([b2_im, b2_re], axis=-1)[:, :, None]
    w_e = (a1 * b1 + a2 * b2).astype(BF16).reshape(G, C * L, 4 * P)

    cf_re, cf_im = power(j + 1, 0)
    cb_re, cb_im = power(L - j, 1)
    crt, cit = jnp.transpose(cr, (0, 2, 1)), jnp.transpose(ci, (0, 2, 1))
    c1 = jnp.concatenate([crt, crt, -crt, -crt], axis=1)[..., None]
    c2 = jnp.concatenate([-cit, -cit, -cit, -cit], axis=1)[..., None]
    q1 = jnp.concatenate([cf_re, cb_re, cf_im, cb_im], axis=1)[:, :, None]
    q2 = jnp.concatenate([cf_im, cb_im, cf_re, cb_re], axis=1)[:, :, None]
    w_c = (c1 * q1 + c2 * q2).astype(BF16).reshape(G, 4 * P, C * L)

    ar = jnp.concatenate([sq[-1][0][0], sq[-1][0][1]], axis=-1)
    ai = jnp.concatenate([sq[-1][1][0], sq[-1][1][1]], axis=-1)
    rows = []
    step = 1
    while step < nc:
        rows += [ar, ai]
        ar, ai = ar * ar - ai * ai, 2.0 * ar * ai
        step *= 2
    if not rows:
        rows = [ar, ai]
    apow = jnp.stack(rows, axis=1)
    return kfull, w_e, w_c, apow


def _s5_kernel(u_ref, kf_ref, we_ref, wc_ref, ap_ref, o_ref, w_s, *, nc):
    C, P, L = S5_GROUP, S5_P, S5_CHUNK

    def build(cp, carry):
        r0 = pl.multiple_of(cp * L, L)
        kv = kf_ref[0, pl.ds(pl.multiple_of(cp * C, C), C), :]
        for c in range(C):
            vb = jnp.broadcast_to(kv[c:c + 1, :], (L, 2 * L))
            blk = pltpu.roll(vb, L + 1, 1, stride=1, stride_axis=0)[:, :L]
            w_s[pl.ds(r0, L), c * L:(c + 1) * L] = blk.astype(BF16)
        return carry
    lax.fori_loop(0, C, build, 0)

    u = jnp.concatenate([u_ref[c] for c in range(C)], axis=-1)
    y = jnp.dot(u, w_s[...], preferred_element_type=F32)
    e = jnp.dot(u, we_ref[0], preferred_element_type=F32)
    re, im = e[:, :2 * P], e[:, 2 * P:]
    R = re.shape[0]
    k = lax.broadcasted_iota(jnp.int32, (R, 2 * P), 0) % nc
    is_f = lax.broadcasted_iota(jnp.int32, (R, 2 * P), 1) < P

    def shifted(v, dist):
        prev = jnp.where(k >= dist, pltpu.roll(v, dist, 0), 0.0)
        nxt = jnp.where(k < nc - dist, pltpu.roll(v, R - dist, 0), 0.0)
        return jnp.where(is_f, prev, nxt)

    dist, i = 1, 0
    while dist < nc:
        ar = ap_ref[0, 2 * i:2 * i + 1, :]
        ai = ap_ref[0, 2 * i + 1:2 * i + 2, :]
        s_re, s_im = shifted(re, dist), shifted(im, dist)
        re, im = re + ar * s_re - ai * s_im, im + ar * s_im + ai * s_re
        dist, i = dist * 2, i + 1
    if nc > 1:
        st = jnp.concatenate([shifted(re, 1), shifted(im, 1)], axis=-1).astype(BF16)
        y = y + jnp.dot(st, wc_ref[0], preferred_element_type=F32)
    for c in range(C):
        o_ref[c] = y[:, c * L:(c + 1) * L].astype(o_ref.dtype)


def _s5_core(ut3, kfull, w_e, w_c, apow, nc):
    _, R, L = ut3.shape
    C, G, P = S5_GROUP, S5_NG, S5_P
    n_ap = apow.shape[1]
    return pl.pallas_call(
        functools.partial(_s5_kernel, nc=nc),
        grid=(G,),
        in_specs=[
            pl.BlockSpec((C, R, L), lambda g: (g, 0, 0)),
            pl.BlockSpec((1, C * C, 2 * L), lambda g: (g, 0, 0)),
            pl.BlockSpec((1, C * L, 4 * P), lambda g: (g, 0, 0)),
            pl.BlockSpec((1, 4 * P, C * L), lambda g: (g, 0, 0)),
            pl.BlockSpec((1, n_ap, 2 * P), lambda g: (g, 0, 0)),
        ],
        out_specs=pl.BlockSpec((C, R, L), lambda g: (g, 0, 0)),
        out_shape=jax.ShapeDtypeStruct((D_S5, R, L), BF16),
        scratch_shapes=[pltpu.VMEM((C * L, C * L), BF16)],
        compiler_params=_params(dimension_semantics=("arbitrary",)),
        name="s5_core",
    )(ut3, kfull, w_e, w_c, apow)


def _log_sigmoid(x):
    return jnp.minimum(x, 0.0) - jnp.log(1.0 + jnp.exp(-jnp.abs(x)))


def _mlstm_gate_tables(gp, nc):
    L, H = ML_CHUNK, ML_HEADS
    NJ = 2 * H
    R = nc * NJ
    li = gp[:, :NJ, :].reshape(R, L)
    lf = _log_sigmoid(gp[:, NJ:, :]).reshape(R, L)
    is_f = (lax.broadcasted_iota(jnp.int32, (R, L), 0) % NJ) < H
    lane = lax.broadcasted_iota(jnp.int32, (R, L), 1)

    def scan(x, op, fill):
        dist = 1
        while dist < L:
            pre = jnp.where(lane >= dist, pltpu.roll(x, dist, 1), fill)
            suf = jnp.where(lane < L - dist, pltpu.roll(x, L - dist, 1), fill)
            x = op(x, jnp.where(is_f, pre, suf))
            dist *= 2
        return x

    b = scan(lf, jnp.add, 0.0)
    e = li - b
    c = scan(e, jnp.maximum, -jnp.inf)
    is_f_col = is_f[:, :1]
    btot = jnp.where(is_f_col, b[:, L - 1:], b[:, :1])
    g = btot - b + li
    gmax = jnp.max(g, axis=1, keepdims=True)

    f8 = lax.broadcasted_iota(jnp.int32, (NJ, 1), 0) < H
    blk = lambda v, kk: v[kk * NJ:(kk + 1) * NJ]
    m = jnp.zeros((NJ, 1), F32)
    seen = []
    for i in range(nc):
        seen.append(m)
        m = jnp.maximum(jnp.where(f8, blk(btot, i), blk(btot, nc - 1 - i)) + m,
                        jnp.where(f8, blk(gmax, i), blk(gmax, nc - 1 - i)))
    m_start = jnp.concatenate([jnp.where(f8, seen[kk], seen[nc - 1 - kk]) for kk in range(nc)], axis=0)
    m_next = jnp.maximum(btot + m_start, gmax)

    cm = jnp.maximum(m_start, c)
    w_inter = jnp.exp(m_start - cm)
    floor = jnp.exp(-b - cm)
    w_state = jnp.exp(g - m_next)
    decay = jnp.exp(btot + m_start - m_next)
    return e, cm, w_inter, floor, w_state, decay


def _mlstm_kernel(xm_ref, op_ref, g_ref, cw_ref, cb_ref, wq_ref, wk_ref, wvt_ref, hn_ref, sk_ref, gb_ref,
                  y_ref, xp_s, xc_s, q_s, k_s, vt_s, hf_s, hb_s, e_s, ws_s, dec_s, col_s, c_s):
    S = xm_ref.shape[0]
    L, H, DH = ML_CHUNK, ML_HEADS, ML_DH
    NJ = 2 * H
    nc = S // L
    half = CONV_K // 2

    xp_s[0:SUBLANES, :] = jnp.zeros((SUBLANES, D_ML), F32)
    xp_s[SUBLANES + S:, :] = jnp.zeros((SUBLANES, D_ML), F32)
    xp_s[SUBLANES:SUBLANES + S, :] = xm_ref[...].astype(F32)
    acc = cb_ref[...] + xp_s[SUBLANES - half:SUBLANES - half + S, :] * cw_ref[0:1, :]
    for i in range(1, CONV_K):
        acc = acc + xp_s[SUBLANES - half + i:SUBLANES - half + i + S, :] * cw_ref[i:i + 1, :]
    xc_s[...] = (acc * _sigmoid(acc)).astype(BF16)

    xc = xc_s[...]
    xmb = xm_ref[...]
    for p in range(H // 2):
        sl = slice(p * 2 * DH, (p + 1) * 2 * DH)
        q_s[:, sl] = jnp.dot(xc[:, sl], wq_ref[p], preferred_element_type=F32).astype(BF16)
        k_s[:, sl] = jnp.dot(xc[:, sl], wk_ref[p], preferred_element_type=F32).astype(BF16)
        vt = lax.dot_general(wvt_ref[p], xmb[:, sl], _NT, preferred_element_type=F32)
        for kk in range(nc):
            vt_s[kk, sl, :] = vt[:, kk * L:(kk + 1) * L].astype(BF16)

    gp = g_ref[0] + gb_ref[...][None]
    e, cm, w_inter, floor, w_state, decay = _mlstm_gate_tables(gp, nc)
    e_s[...] = e.reshape(nc, NJ, L)
    ws_s[...] = w_state.reshape(nc, NJ, L)
    dec_s[...] = jnp.broadcast_to(decay, (nc * NJ, DH)).reshape(nc, NJ, DH)
    pad = jnp.zeros((L - 3 * NJ, L), F32)
    for kk in range(nc):
        rs = slice(kk * NJ, (kk + 1) * NJ)
        col_s[kk] = jnp.concatenate([cm[rs], w_inter[rs], floor[rs], pad], axis=0).T

    c_s[...] = jnp.zeros_like(c_s)
    ti = lax.broadcasted_iota(jnp.int32, (L, L), 0)
    si = lax.broadcasted_iota(jnp.int32, (L, L), 1)
    tri_f, tri_b = si <= ti, si >= ti
    ones_rows = jnp.ones((DH, L), BF16)

    def body(i, carry):
        for rev in (False, True):
            ci = (nc - 1 - i) if rev else i
            r0 = pl.multiple_of(ci * L, L)
            cols, e_ch, ws_ch, dec_ch = col_s[ci], e_s[ci], ws_s[ci], dec_s[ci]
            for h in range(H):
                j = (H if rev else 0) + h
                cs = slice(h * DH, (h + 1) * DH)
                q = q_s[pl.ds(r0, L), cs]
                k = k_s[pl.ds(r0, L), cs]
                vt_aug = jnp.concatenate([vt_s[ci, cs, :], ones_rows], axis=0)
                c_aug = c_s[j]
                w = jnp.where(tri_b if rev else tri_f,
                              jnp.exp(e_ch[j:j + 1, :] - cols[:, j:j + 1]), 0.0)
                s = lax.dot_general(q, k, _NT, preferred_element_type=F32) * w
                qw = (q.astype(F32) * cols[:, NJ + j:NJ + j + 1]).astype(BF16)
                lhs = jnp.concatenate([s.astype(BF16), qw], axis=-1)
                rhs_t = jnp.concatenate([vt_aug, c_aug.astype(BF16)], axis=-1)
                r = lax.dot_general(lhs, rhs_t, _NT, preferred_element_type=F32)
                hh = r[:, :DH] / jnp.maximum(jnp.abs(r[:, DH:]), cols[:, 2 * NJ + j:2 * NJ + j + 1])
                vw = vt_aug * ws_ch[j:j + 1, :].astype(BF16)
                c_s[j] = dec_ch[j:j + 1, :] * c_aug + jnp.dot(vw, k, preferred_element_type=F32)
                if rev:
                    hb_s[pl.ds(r0, L), cs] = hh
                else:
                    hf_s[pl.ds(r0, L), cs] = hh
        return carry

    lax.fori_loop(0, nc, body, 0)

    xcf = xc_s[...].astype(F32)
    o = _sigmoid(op_ref[...].astype(F32))
    for h in range(H):
        cs = slice(h * DH, (h + 1) * DH)
        hh = hf_s[:, cs] + hb_s[:, cs]
        mu = jnp.mean(hh, axis=-1, keepdims=True)
        var = jnp.mean(jnp.square(hh - mu), axis=-1, keepdims=True)
        hn = (hh - mu) * lax.rsqrt(var + EPS) * hn_ref[:, cs]
        y_ref[:, cs] = (o[:, cs] * (hn + sk_ref[:, cs] * xcf[:, cs])).astype(y_ref.dtype)


def _blockdiag2(w):
    H, DH, _ = w.shape
    z = jnp.zeros((DH, DH), w.dtype)
    out = []
    for p in range(H // 2):
        a, b = w[2 * p], w[2 * p + 1]
        out.append(jnp.concatenate([jnp.concatenate([a, z], 1), jnp.concatenate([z, b], 1)], 0))
    return jnp.stack(out)


def _mlstm(xm, op, gt4, conv_w, conv_b, wq, wk, wv, head_norm, skip, gate_bias, B, S):
    L, H, DH = ML_CHUNK, ML_HEADS, ML_DH
    NJ = 2 * H
    nc = S // L
    wq2 = _blockdiag2(wq.astype(F32)).astype(BF16)
    wk2 = _blockdiag2(wk.astype(F32) * (DH ** -0.5)).astype(BF16)
    wvt2 = jnp.transpose(_blockdiag2(wv.astype(F32)), (0, 2, 1)).astype(BF16)
    gb = gate_bias.astype(F32).reshape(N_GATE_COLS, 1)
    row = lambda v: v.astype(F32).reshape(1, D_ML)
    full = lambda shape: pl.BlockSpec(shape, lambda b: (0,) * len(shape))
    return pl.pallas_call(
        _mlstm_kernel,
        grid=(B,),
        in_specs=[
            pl.BlockSpec((S, D_ML), lambda b: (b, 0)),
            pl.BlockSpec((S, D_ML), lambda b: (b, 0)),
            pl.BlockSpec((1, nc, N_GATE_COLS, L), lambda b: (b, 0, 0, 0)),
            full((CONV_K, D_ML)), full((1, D_ML)),
            full((H // 2, 2 * DH, 2 * DH)), full((H // 2, 2 * DH, 2 * DH)), full((H // 2, 2 * DH, 2 * DH)),
            full((1, D_ML)), full((1, D_ML)), full((N_GATE_COLS, 1)),
        ],
        out_specs=pl.BlockSpec((S, D_ML), lambda b: (b, 0)),
        out_shape=jax.ShapeDtypeStruct((B * S, D_ML), BF16),
        scratch_shapes=[
            pltpu.VMEM((S + 2 * SUBLANES, D_ML), F32),
            pltpu.VMEM((S, D_ML), BF16),
            pltpu.VMEM((S, D_ML), BF16),
            pltpu.VMEM((S, D_ML), BF16),
            pltpu.VMEM((nc, D_ML, L), BF16),
            pltpu.VMEM((S, D_ML), F32),
            pltpu.VMEM((S, D_ML), F32),
            pltpu.VMEM((nc, NJ, L), F32),
            pltpu.VMEM((nc, NJ, L), F32),
            pltpu.VMEM((nc, NJ, DH), F32),
            pltpu.VMEM((nc, L, LANES), F32),
            pltpu.VMEM((NJ, 2 * DH, DH), F32),
        ],
        compiler_params=_params(dimension_semantics=("arbitrary",)),
        name="mlstm",
    )(xm, op, gt4, conv_w.astype(F32), row(conv_b), wq2, wk2, wvt2, row(head_norm), row(skip), gb)


_TN = (((0,), (0,)), ((), ()))


def _mixout_kernel(x_ref, yt_ref, yml_ref, wgt_ref, bg_ref, wo1_ref, wo2_ref, gpost_ref, gffn_ref,
                   x1_ref, h2_ref):
    g = _gelu_tanh(yt_ref[...].astype(F32))
    z = jnp.dot(wgt_ref[...], g.astype(BF16), preferred_element_type=F32) + bg_ref[...]
    s5 = (g * _sigmoid(z)).astype(BF16)
    acc = (lax.dot_general(s5, wo1_ref[...], _TN, preferred_element_type=F32)
           + jnp.dot(yml_ref[...], wo2_ref[...], preferred_element_type=F32))
    x1 = x_ref[...] + _rms(acc) * gpost_ref[...]
    x1_ref[...] = x1
    h2_ref[...] = (_rms(x1) * gffn_ref[...]).astype(BF16)


def _mixout(x2, yt, yml, w_glu, b_glu, w_out, g_post, g_ffn, tn):
    T = x2.shape[0]
    wgt = w_glu.T.astype(BF16)
    bg = b_glu.astype(F32).reshape(D_S5, 1)
    wo1 = w_out[:D_S5].astype(BF16)
    wo2 = w_out[D_S5:].astype(BF16)
    full = lambda shape: pl.BlockSpec(shape, lambda i: (0,) * len(shape))
    return pl.pallas_call(
        _mixout_kernel,
        grid=(T // tn,),
        in_specs=[
            pl.BlockSpec((tn, D_MODEL), lambda i: (i, 0)),
            pl.BlockSpec((D_S5, tn), lambda i: (0, i)),
            pl.BlockSpec((tn, D_ML), lambda i: (i, 0)),
            full((D_S5, D_S5)), full((D_S5, 1)),
            full((D_S5, D_MODEL)), full((D_ML, D_MODEL)),
            full((1, D_MODEL)), full((1, D_MODEL)),
        ],
        out_specs=[
            pl.BlockSpec((tn, D_MODEL), lambda i: (i, 0)),
            pl.BlockSpec((tn, D_MODEL), lambda i: (i, 0)),
        ],
        out_shape=[
            jax.ShapeDtypeStruct((T, D_MODEL), F32),
            jax.ShapeDtypeStruct((T, D_MODEL), BF16),
        ],
        compiler_params=_params(dimension_semantics=("arbitrary",)),
        name="mixout",
    )(x2, yt, yml, wgt, bg, wo1, wo2, g_post.astype(F32).reshape(1, D_MODEL),
      g_ffn.astype(F32).reshape(1, D_MODEL))


def _ffn_kernel(x1_ref, h2_ref, wg_ref, wu_ref, wd_ref, gn_ref, o_ref):
    h = h2_ref[...]
    acc = jnp.zeros((h.shape[0], D_MODEL), F32)
    for c in range(D_FF // FF_CHUNK):
        sl = slice(c * FF_CHUNK, (c + 1) * FF_CHUNK)
        gg = jnp.dot(h, wg_ref[:, sl], preferred_element_type=F32)
        uu = jnp.dot(h, wu_ref[:, sl], preferred_element_type=F32)
        a = (gg * _sigmoid(gg) * uu).astype(BF16)
        acc = acc + jnp.dot(a, wd_ref[sl, :], preferred_element_type=F32)
    o_ref[...] = x1_ref[...] + _rms(acc) * gn_ref[...]


def _ffn(x1, h2, w_gate, w_up, w_down, g_post, tm):
    T = x1.shape[0]
    const = lambda shape: pl.BlockSpec(shape, lambda i: (0,) * len(shape), pipeline_mode=pl.Buffered(1))
    return pl.pallas_call(
        _ffn_kernel,
        grid=(T // tm,),
        in_specs=[
            pl.BlockSpec((tm, D_MODEL), lambda i: (i, 0)),
            pl.BlockSpec((tm, D_MODEL), lambda i: (i, 0)),
            const((D_MODEL, D_FF)), const((D_MODEL, D_FF)), const((D_FF, D_MODEL)),
            const((1, D_MODEL)),
        ],
        out_specs=pl.BlockSpec((tm, D_MODEL), lambda i: (i, 0)),
        out_shape=jax.ShapeDtypeStruct((T, D_MODEL), F32),
        compiler_params=_params(dimension_semantics=("arbitrary",)),
        name="ffn",
    )(x1, h2, w_gate.astype(BF16), w_up.astype(BF16), w_down.astype(BF16),
      g_post.astype(F32).reshape(1, D_MODEL))


def _tile(T, want):
    t = min(T, want)
    assert T % t == 0
    return t


def _layer(x, norm_mix_pre, norm_mix_post, norm_ffn_pre, norm_ffn_post, w_in, ml_gate_bias,
           s5_lam_re, s5_lam_im, s5_log_dt, s5_b_re, s5_b_im, s5_c_re, s5_c_im, s5_d,
           s5_w_glu, s5_b_glu, ml_conv_w, ml_conv_b, ml_wq, ml_wk, ml_wv, ml_head_norm,
           ml_skip, w_out, w_gate, w_up, w_down):
    B, S, D = x.shape
    assert D == D_MODEL and S % S5_CHUNK == 0 and S % ML_CHUNK == 0
    T = B * S
    nc5 = S // S5_CHUNK
    assert nc5 & (nc5 - 1) == 0, "the chunk scan assumes a power-of-two chunk count"
    x2 = x.reshape(T, D)

    ut, gt, xm, op = _inproj(x2, norm_mix_pre, w_in, _tile(T, 1024))

    kfull, w_e, w_c, apow = _s5_weights(s5_lam_re, s5_lam_im, s5_log_dt, s5_b_re, s5_b_im,
                                        s5_c_re, s5_c_im, s5_d, nc5)
    yt3 = _s5_core(ut.reshape(D_S5, T // S5_CHUNK, S5_CHUNK), kfull, w_e, w_c, apow, nc5)
    yt = yt3.reshape(D_S5, T)

    ncm = S // ML_CHUNK
    gt4 = jnp.transpose(gt.reshape(N_GATE_COLS, B, ncm, ML_CHUNK), (1, 2, 0, 3))
    yml = _mlstm(xm, op, gt4, ml_conv_w, ml_conv_b, ml_wq, ml_wk, ml_wv, ml_head_norm, ml_skip,
                 ml_gate_bias, B, S)

    x1, h2 = _mixout(x2, yt, yml, s5_w_glu, s5_b_glu, w_out, norm_mix_post, norm_ffn_pre, _tile(T, 512))
    out = _ffn(x1, h2, w_gate, w_up, w_down, norm_ffn_post, _tile(T, 512))
    return out.reshape(B, S, D)


def kernel(x, norm_mix_pre, norm_mix_post, norm_ffn_pre, norm_ffn_post, w_in, ml_gate_bias, s5_lam_re, s5_lam_im, s5_log_dt, s5_b_re, s5_b_im, s5_c_re, s5_c_im, s5_d, s5_w_glu, s5_b_glu, ml_conv_w, ml_conv_b, ml_wq, ml_wk, ml_wv, ml_head_norm, ml_skip, w_out, w_gate, w_up, w_down):
    depth = w_in.shape[0]
    for l in range(depth):
        x = _layer(
            x, norm_mix_pre[l], norm_mix_post[l], norm_ffn_pre[l], norm_ffn_post[l], w_in[l],
            ml_gate_bias[l], s5_lam_re[l], s5_lam_im[l], s5_log_dt[l], s5_b_re[l], s5_b_im[l],
            s5_c_re[l], s5_c_im[l], s5_d[l], s5_w_glu[l], s5_b_glu[l], ml_conv_w[l], ml_conv_b[l],
            ml_wq[l], ml_wk[l], ml_wv[l], ml_head_norm[l], ml_skip[l], w_out[l], w_gate[l],
            w_up[l], w_down[l])
    return x
```

```python
import functools
import math

import numpy as np
import jax
import jax.numpy as jnp
from jax import lax
from jax.experimental import pallas as pl
from jax.experimental.pallas import tpu as pltpu

D_MODEL = 1024
D_S5 = 512
D_ML = 512
S5_GROUP = 16
S5_NG = D_S5 // S5_GROUP
S5_P = 64
ML_HEADS = 4
ML_DH = D_ML // ML_HEADS
CONV_K = 5
N_GATES = 4
N_GATE_COLS = N_GATES * ML_HEADS
D_FF = 256 * math.ceil(8 * D_MODEL / 3 / 256)
EPS = 1e-6

LANES = 128
SUBLANES = 8
S5_CHUNK = LANES
S5_SLAB = 2
ML_CHUNK = LANES
FF_CHUNK = 256
MIX_SUB = 256
VMEM_LIMIT = 56 * 1024 * 1024

F32 = jnp.float32
BF16 = jnp.bfloat16

_NT = (((1,), (1,)), ((), ()))


def _rms(x):
    return x * lax.rsqrt(jnp.mean(x * x, axis=-1, keepdims=True) + EPS)


def _sigmoid(x):
    return 0.5 * jnp.tanh(0.5 * x) + 0.5


def _gelu_tanh(x):
    c = math.sqrt(2.0 / math.pi)
    return 0.5 * x * (1.0 + jnp.tanh(c * (x + 0.044715 * (x * x * x))))


def _params(**kw):
    return pltpu.CompilerParams(vmem_limit_bytes=VMEM_LIMIT, **kw)


def _inproj_kernel(x_ref, g_ref, wt_ref, w_ref, ut_ref, gt_ref, xm_ref, op_ref):
    x = x_ref[...]
    h = (_rms(x) * g_ref[...]).astype(BF16)
    st = lax.dot_general(wt_ref[...], h, _NT, preferred_element_type=F32)
    ut_ref[...] = st[:D_S5].astype(BF16)
    gt_ref[...] = st[D_S5:]
    p = jnp.dot(h, w_ref[...], preferred_element_type=F32)
    xm_ref[...] = p[:, :D_ML].astype(BF16)
    op_ref[...] = p[:, D_ML:].astype(BF16)


def _inproj(x2, g_pre, w_in, tm):
    T = x2.shape[0]
    wt = jnp.concatenate([w_in[:, :D_S5], w_in[:, D_S5 + 2 * D_ML:]], axis=1).T.astype(BF16)
    w = w_in[:, D_S5:D_S5 + 2 * D_ML].astype(BF16)
    n_t = D_S5 + N_GATE_COLS
    return pl.pallas_call(
        _inproj_kernel,
        grid=(T // tm,),
        in_specs=[
            pl.BlockSpec((tm, D_MODEL), lambda i: (i, 0)),
            pl.BlockSpec((1, D_MODEL), lambda i: (0, 0)),
            pl.BlockSpec((n_t, D_MODEL), lambda i: (0, 0)),
            pl.BlockSpec((D_MODEL, 2 * D_ML), lambda i: (0, 0)),
        ],
        out_specs=[
            pl.BlockSpec((D_S5, tm), lambda i: (0, i)),
            pl.BlockSpec((N_GATE_COLS, tm), lambda i: (0, i)),
            pl.BlockSpec((tm, D_ML), lambda i: (i, 0)),
            pl.BlockSpec((tm, D_ML), lambda i: (i, 0)),
        ],
        out_shape=[
            jax.ShapeDtypeStruct((D_S5, T), BF16),
            jax.ShapeDtypeStruct((N_GATE_COLS, T), F32),
            jax.ShapeDtypeStruct((T, D_ML), BF16),
            jax.ShapeDtypeStruct((T, D_ML), BF16),
        ],
        compiler_params=_params(dimension_semantics=("arbitrary",)),
        name="inproj",
    )(x2, g_pre.reshape(1, D_MODEL), wt, w)


def _s5_weights(lam_re, lam_im, log_dt, b_re, b_im, c_re, c_im, d, nc):
    L, G, P, C = S5_CHUNK, S5_NG, S5_P, S5_GROUP
    hi = lax.Precision.HIGHEST
    lr, li = lam_re.astype(F32), lam_im.astype(F32)
    dt = jnp.exp(log_dt.astype(F32))[..., None]
    mag = jnp.exp(lr * dt)
    bar_re, bar_im = mag * jnp.cos(li * dt), mag * jnp.sin(li * dt)
    xr, xi = bar_re - 1.0, bar_im
    den = lr * lr + li * li
    fr = (xr * lr + xi * li) / den
    fi = (xi * lr - xr * li) / den
    br = jnp.transpose(b_re.astype(F32), (0, 2, 1))[None]
    bi = jnp.transpose(b_im.astype(F32), (0, 2, 1))[None]
    bb_re = fr[:, :, None, :] * br - fi[:, :, None, :] * bi
    bb_im = fr[:, :, None, :] * bi + fi[:, :, None, :] * br
    cr, ci = c_re.astype(F32), c_im.astype(F32)

    sq = [(bar_re, bar_im)]
    for _ in range(L.bit_length() - 1):
        r, i = sq[-1]
        sq.append((r * r - i * i, 2.0 * r * i))

    def power(tau, direction, valid=None):
        tau = np.asarray(tau)
        re = jnp.ones((G, P, tau.size), F32)
        im = jnp.zeros((G, P, tau.size), F32)
        for bit, (s_re, s_im) in enumerate(sq):
            sel = jnp.asarray(((tau >> bit) & 1).astype(bool))
            f_re = jnp.where(sel, s_re[direction][..., None], 1.0)
            f_im = jnp.where(sel, s_im[direction][..., None], 0.0)
            re, im = re * f_re - im * f_im, re * f_im + im * f_re
        if valid is not None:
            ok = jnp.asarray(np.asarray(valid))
            re, im = jnp.where(ok, re, 0.0), jnp.where(ok, im, 0.0)
        return re, im

    m = np.arange(2 * L)
    kf_re, kf_im = power(np.maximum(m - (L - 1), 0), 0, (m >= L - 1) & (m <= 2 * L - 2))
    kb_re, kb_im = power(np.maximum((L - 1) - m, 0), 1, m <= L - 1)
    tab = jnp.stack([jnp.concatenate([kf_re, kf_im], axis=1),
                     jnp.concatenate([kb_re, kb_im], axis=1)])
    x_re = cr[None, :, None] * bb_re[:, :, :, None] - ci[None, :, None] * bb_im[:, :, :, None]
    x_im = -cr[None, :, None] * bb_im[:, :, :, None] - ci[None, :, None] * bb_re[:, :, :, None]
    xx = jnp.concatenate([x_re, x_im], axis=-1)
    kfull = jnp.einsum('dgecp,dgpm->gecm', xx, tab, precision=hi)
    at_zero = jnp.asarray(m == L - 1)
    skip = d.astype(F32).reshape(G, 1, C, 1) * jnp.eye(C, dtype=F32)[None, :, :, None]
    kfull = (kfull + jnp.where(at_zero, skip, 0.0)).reshape(G, C * C, 2 * L)

    j = np.arange(L)
    ef_re, ef_im = power(L - 1 - j, 0)
    eb_re, eb_im = power(j, 1)
    a_re = jnp.transpose(jnp.concatenate([ef_re, eb_re], axis=1), (0, 2, 1))
    a_im = jnp.transpose(jnp.concatenate([ef_im, eb_im], axis=1), (0, 2, 1))
    b2_re = jnp.concatenate([bb_re[0], bb_re[1]], axis=-1)
    b2_im = jnp.concatenate([bb_im[0], bb_im[1]], axis=-1)
    a1 = jnp.concatenate([a_re, a_re], axis=-1)[:, None]
    a2 = jnp.concatenate([-a_im, a_im], axis=-1)[:, None]
    b1 = jnp.concatenate([b2_re, b2_im], axis=-1)[:, :, None]
    b2 = jnp.concatenate([b2_im, b2_re], axis=-1)[:, :, None]
    w_e = (a1 * b1 + a2 * b2).astype(BF16)

    cf_re, cf_im = power(j + 1, 0)
    cb_re, cb_im = power(L - j, 1)
    crt, cit = jnp.transpose(cr, (0, 2, 1)), jnp.transpose(ci, (0, 2, 1))
    c1 = jnp.concatenate([crt, crt, -crt, -crt], axis=1)[..., None]
    c2 = jnp.concatenate([-cit, -cit, -cit, -cit], axis=1)[..., None]
    q1 = jnp.concatenate([cf_re, cb_re, cf_im, cb_im], axis=1)[:, :, None]
    q2 = jnp.concatenate([cf_im, cb_im, cf_re, cb_re], axis=1)[:, :, None]
    w_c = (c1 * q1 + c2 * q2).astype(BF16).reshape(G, 4 * P, C * L)

    ar = jnp.concatenate([sq[-1][0][0], sq[-1][0][1]], axis=-1)
    ai = jnp.concatenate([sq[-1][1][0], sq[-1][1][1]], axis=-1)
    rows = []
    step = 1
    while step < nc:
        rows += [ar, ai]
        ar, ai = ar * ar - ai * ai, 2.0 * ar * ai
        step *= 2
    if not rows:
        rows = [ar, ai]
    apow = jnp.stack(rows, axis=1)
    return kfull, w_e, w_c, apow


def _s5_kernel(u_ref, kf_ref, we_ref, wc_ref, ap_ref, o_ref, w_s, *, nc):
    C, P, L = S5_GROUP, S5_P, S5_CHUNK
    u = jnp.concatenate([u_ref[c] for c in range(C)], axis=-1)

    e = jnp.dot(u, we_ref[0].reshape(C * L, 4 * P), preferred_element_type=F32)
    re, im = e[:, :2 * P], e[:, 2 * P:]
    R = re.shape[0]
    k = lax.broadcasted_iota(jnp.int32, (R, 2 * P), 0) % nc
    is_f = lax.broadcasted_iota(jnp.int32, (R, 2 * P), 1) < P

    def shifted(v, dist):
        prev = jnp.where(k >= dist, pltpu.roll(v, dist, 0), 0.0)
        nxt = jnp.where(k < nc - dist, pltpu.roll(v, R - dist, 0), 0.0)
        return jnp.where(is_f, prev, nxt)

    dist, i = 1, 0
    while dist < nc:
        ar = ap_ref[0, 2 * i:2 * i + 1, :]
        ai = ap_ref[0, 2 * i + 1:2 * i + 2, :]
        s_re, s_im = shifted(re, dist), shifted(im, dist)
        re, im = re + ar * s_re - ai * s_im, im + ar * s_im + ai * s_re
        dist, i = dist * 2, i + 1
    st = jnp.concatenate([shifted(re, 1), shifted(im, 1)], axis=-1).astype(BF16)

    for s in range(C // S5_SLAB):
        for c in range(s * S5_SLAB, (s + 1) * S5_SLAB):
            for cp in range(C):
                vb = jnp.broadcast_to(kf_ref[0, cp * C + c:cp * C + c + 1, :], (L, 2 * L))
                blk = pltpu.roll(vb, L + 1, 1, stride=1, stride_axis=0)[:, :L]
                w_s[cp * L:(cp + 1) * L, c * L:(c + 1) * L] = blk.astype(BF16)
        cols = slice(s * S5_SLAB * L, (s + 1) * S5_SLAB * L)
        y = jnp.dot(u, w_s[:, cols], preferred_element_type=F32)
        if nc > 1:
            y = y + jnp.dot(st, wc_ref[0, :, cols], preferred_element_type=F32)
        for c in range(S5_SLAB):
            o_ref[:, s * S5_SLAB + c, :] = y[:, c * L:(c + 1) * L].astype(o_ref.dtype)


def _s5_core(ut3, kfull, w_e, w_c, apow, nc):
    _, R, L = ut3.shape
    C, G, P = S5_GROUP, S5_NG, S5_P
    n_ap = apow.shape[1]
    return pl.pallas_call(
        functools.partial(_s5_kernel, nc=nc),
        grid=(G,),
        in_specs=[
            pl.BlockSpec((C, R, L), lambda g: (g, 0, 0)),
            pl.BlockSpec((1, C * C, 2 * L), lambda g: (g, 0, 0)),
            pl.BlockSpec((1, C, L, 4 * P), lambda g: (g, 0, 0, 0)),
            pl.BlockSpec((1, 4 * P, C * L), lambda g: (g, 0, 0)),
            pl.BlockSpec((1, n_ap, 2 * P), lambda g: (g, 0, 0)),
        ],
        out_specs=pl.BlockSpec((R, C, L), lambda g: (0, g, 0)),
        out_shape=jax.ShapeDtypeStruct((R, D_S5, L), F32),
        scratch_shapes=[pltpu.VMEM((C * L, C * L), BF16)],
        compiler_params=_params(dimension_semantics=("arbitrary",)),
        name="s5_core",
    )(ut3, kfull, w_e, w_c, apow)


def _log_sigmoid(x):
    return jnp.minimum(x, 0.0) - jnp.log(1.0 + jnp.exp(-jnp.abs(x)))


def _mlstm_gate_tables(gp, nc):
    L, H = ML_CHUNK, ML_HEADS
    NJ = 2 * H
    R = nc * NJ
    li = gp[:, :NJ, :].reshape(R, L)
    lf = _log_sigmoid(gp[:, NJ:, :]).reshape(R, L)
    is_f = (lax.broadcasted_iota(jnp.int32, (R, L), 0) % NJ) < H
    lane = lax.broadcasted_iota(jnp.int32, (R, L), 1)

    def scan(x, op, fill):
        dist = 1
        while dist < L:
            pre = jnp.where(lane >= dist, pltpu.roll(x, dist, 1), fill)
            suf = jnp.where(lane < L - dist, pltpu.roll(x, L - dist, 1), fill)
            x = op(x, jnp.where(is_f, pre, suf))
            dist *= 2
        return x

    b = scan(lf, jnp.add, 0.0)
    e = li - b
    c = scan(e, jnp.maximum, -jnp.inf)
    is_f_col = is_f[:, :1]
    btot = jnp.where(is_f_col, b[:, L - 1:], b[:, :1])
    g = btot - b + li
    gmax = jnp.max(g, axis=1, keepdims=True)

    f8 = lax.broadcasted_iota(jnp.int32, (NJ, 1), 0) < H
    blk = lambda v, kk: v[kk * NJ:(kk + 1) * NJ]
    m = jnp.zeros((NJ, 1), F32)
    seen = []
    for i in range(nc):
        seen.append(m)
        m = jnp.maximum(jnp.where(f8, blk(btot, i), blk(btot, nc - 1 - i)) + m,
                        jnp.where(f8, blk(gmax, i), blk(gmax, nc - 1 - i)))
    m_start = jnp.concatenate([jnp.where(f8, seen[kk], seen[nc - 1 - kk]) for kk in range(nc)], axis=0)
    m_next = jnp.maximum(btot + m_start, gmax)

    cm = jnp.maximum(m_start, c)
    w_inter = jnp.exp(m_start - cm)
    floor = jnp.exp(-b - cm)
    w_state = jnp.exp(g - m_next)
    decay = jnp.exp(btot + m_start - m_next)
    return e, cm, w_inter, floor, w_state, decay


def _mlstm_kernel(xm_ref, op_ref, g_ref, cw_ref, cb_ref, wq_ref, wk_ref, wvt_ref, hn_ref, sk_ref, gb_ref,
                  y_ref, xp_s, xc_s, q_s, k_s, vt_s, hf_s, hb_s, e_s, ws_s, dec_s, col_s, c_s):
    S = xm_ref.shape[0]
    L, H, DH = ML_CHUNK, ML_HEADS, ML_DH
    NJ = 2 * H
    nc = S // L
    half = CONV_K // 2

    xp_s[0:SUBLANES, :] = jnp.zeros((SUBLANES, D_ML), F32)
    xp_s[SUBLANES + S:, :] = jnp.zeros((SUBLANES, D_ML), F32)
    xp_s[SUBLANES:SUBLANES + S, :] = xm_ref[...].astype(F32)
    acc = cb_ref[...] + xp_s[SUBLANES - half:SUBLANES - half + S, :] * cw_ref[0:1, :]
    for i in range(1, CONV_K):
        acc = acc + xp_s[SUBLANES - half + i:SUBLANES - half + i + S, :] * cw_ref[i:i + 1, :]
    xc_s[...] = (acc * _sigmoid(acc)).astype(BF16)

    xc = xc_s[...]
    xmb = xm_ref[...]
    for p in range(H // 2):
        sl = slice(p * 2 * DH, (p + 1) * 2 * DH)
        q_s[:, sl] = jnp.dot(xc[:, sl], wq_ref[p], preferred_element_type=F32).astype(BF16)
        k_s[:, sl] = jnp.dot(xc[:, sl], wk_ref[p], preferred_element_type=F32).astype(BF16)
        vt = lax.dot_general(wvt_ref[p], xmb[:, sl], _NT, preferred_element_type=F32)
        for kk in range(nc):
            vt_s[kk, sl, :] = vt[:, kk * L:(kk + 1) * L].astype(BF16)

    gp = g_ref[0] + gb_ref[...][None]
    e, cm, w_inter, floor, w_state, decay = _mlstm_gate_tables(gp, nc)
    e_s[...] = e.reshape(nc, NJ, L)
    ws_s[...] = w_state.reshape(nc, NJ, L)
    dec_s[...] = jnp.broadcast_to(decay, (nc * NJ, DH)).reshape(nc, NJ, DH)
    pad = jnp.zeros((L - 3 * NJ, L), F32)
    for kk in range(nc):
        rs = slice(kk * NJ, (kk + 1) * NJ)
        col_s[kk] = jnp.concatenate([cm[rs], w_inter[rs], floor[rs], pad], axis=0).T

    c_s[...] = jnp.zeros_like(c_s)
    ti = lax.broadcasted_iota(jnp.int32, (L, L), 0)
    si = lax.broadcasted_iota(jnp.int32, (L, L), 1)
    tri_f, tri_b = si <= ti, si >= ti
    ones_rows = jnp.ones((DH, L), BF16)

    def body(i, carry):
        for rev in (False, True):
            ci = (nc - 1 - i) if rev else i
            r0 = pl.multiple_of(ci * L, L)
            cols, e_ch, ws_ch, dec_ch = col_s[ci], e_s[ci], ws_s[ci], dec_s[ci]
            for h in range(H):
                j = (H if rev else 0) + h
                cs = slice(h * DH, (h + 1) * DH)
                q = q_s[pl.ds(r0, L), cs]
                k = k_s[pl.ds(r0, L), cs]
                vt_aug = jnp.concatenate([vt_s[ci, cs, :], ones_rows], axis=0)
                c_aug = c_s[j]
                w = jnp.where(tri_b if rev else tri_f,
                              jnp.exp(e_ch[j:j + 1, :] - cols[:, j:j + 1]), 0.0)
                s = lax.dot_general(q, k, _NT, preferred_element_type=F32) * w
                qw = (q.astype(F32) * cols[:, NJ + j:NJ + j + 1]).astype(BF16)
                lhs = jnp.concatenate([s.astype(BF16), qw], axis=-1)
                rhs_t = jnp.concatenate([vt_aug, c_aug.astype(BF16)], axis=-1)
                r = lax.dot_general(lhs, rhs_t, _NT, preferred_element_type=F32)
                hh = r[:, :DH] / jnp.maximum(jnp.abs(r[:, DH:]), cols[:, 2 * NJ + j:2 * NJ + j + 1])
                vw = vt_aug * ws_ch[j:j + 1, :].astype(BF16)
                c_s[j] = dec_ch[j:j + 1, :] * c_aug + jnp.dot(vw, k, preferred_element_type=F32)
                if rev:
                    hb_s[pl.ds(r0, L), cs] = hh
                else:
                    hf_s[pl.ds(r0, L), cs] = hh
        return carry

    lax.fori_loop(0, nc, body, 0, unroll=2 if nc % 2 == 0 else 1)

    xcf = xc_s[...].astype(F32)
    o = _sigmoid(op_ref[...].astype(F32))
    for h in range(H):
        cs = slice(h * DH, (h + 1) * DH)
        hh = hf_s[:, cs] + hb_s[:, cs]
        mu = jnp.mean(hh, axis=-1, keepdims=True)
        var = jnp.mean(jnp.square(hh - mu), axis=-1, keepdims=True)
        hn = (hh - mu) * lax.rsqrt(var + EPS) * hn_ref[:, cs]
        y_ref[:, cs] = (o[:, cs] * (hn + sk_ref[:, cs] * xcf[:, cs])).astype(y_ref.dtype)


def _blockdiag2(w):
    H, DH, _ = w.shape
    z = jnp.zeros((DH, DH), w.dtype)
    out = []
    for p in range(H // 2):
        a, b = w[2 * p], w[2 * p + 1]
        out.append(jnp.concatenate([jnp.concatenate([a, z], 1), jnp.concatenate([z, b], 1)], 0))
    return jnp.stack(out)


def _mlstm(xm, op, gt4, conv_w, conv_b, wq, wk, wv, head_norm, skip, gate_bias, B, S):
    L, H, DH = ML_CHUNK, ML_HEADS, ML_DH
    NJ = 2 * H
    nc = S // L
    wq2 = _blockdiag2(wq.astype(F32)).astype(BF16)
    wk2 = _blockdiag2(wk.astype(F32) * (DH ** -0.5)).astype(BF16)
    wvt2 = jnp.transpose(_blockdiag2(wv.astype(F32)), (0, 2, 1)).astype(BF16)
    gb = gate_bias.astype(F32).reshape(N_GATE_COLS, 1)
    row = lambda v: v.astype(F32).reshape(1, D_ML)
    full = lambda shape: pl.BlockSpec(shape, lambda b: (0,) * len(shape))
    return pl.pallas_call(
        _mlstm_kernel,
        grid=(B,),
        in_specs=[
            pl.BlockSpec((S, D_ML), lambda b: (b, 0)),
            pl.BlockSpec((S, D_ML), lambda b: (b, 0)),
            pl.BlockSpec((1, nc, N_GATE_COLS, L), lambda b: (b, 0, 0, 0)),
            full((CONV_K, D_ML)), full((1, D_ML)),
            full((H // 2, 2 * DH, 2 * DH)), full((H // 2, 2 * DH, 2 * DH)), full((H // 2, 2 * DH, 2 * DH)),
            full((1, D_ML)), full((1, D_ML)), full((N_GATE_COLS, 1)),
        ],
        out_specs=pl.BlockSpec((S, D_ML), lambda b: (b, 0)),
        out_shape=jax.ShapeDtypeStruct((B * S, D_ML), BF16),
        scratch_shapes=[
            pltpu.VMEM((S + 2 * SUBLANES, D_ML), F32),
            pltpu.VMEM((S, D_ML), BF16),
            pltpu.VMEM((S, D_ML), BF16),
            pltpu.VMEM((S, D_ML), BF16),
            pltpu.VMEM((nc, D_ML, L), BF16),
            pltpu.VMEM((S, D_ML), F32),
            pltpu.VMEM((S, D_ML), F32),
            pltpu.VMEM((nc, NJ, L), F32),
            pltpu.VMEM((nc, NJ, L), F32),
            pltpu.VMEM((nc, NJ, DH), F32),
            pltpu.VMEM((nc, L, LANES), F32),
            pltpu.VMEM((NJ, 2 * DH, DH), F32),
        ],
        compiler_params=_params(dimension_semantics=("arbitrary",)),
        name="mlstm",
    )(xm, op, gt4, conv_w.astype(F32), row(conv_b), wq2, wk2, wvt2, row(head_norm), row(skip), gb)


_TN = (((0,), (0,)), ((), ()))


def _mixout_kernel(x_ref, yt_ref, yml_ref, wgt_ref, bg_ref, wo1_ref, wo2_ref, gpost_ref, x1_ref):
    for s in range(x_ref.shape[0] // MIX_SUB):
        ts = slice(s * MIX_SUB, (s + 1) * MIX_SUB)
        chunks = range(s * MIX_SUB // S5_CHUNK, (s + 1) * MIX_SUB // S5_CHUNK)
        g = _gelu_tanh(jnp.concatenate([yt_ref[q] for q in chunks], axis=-1).astype(F32))
        z = jnp.dot(wgt_ref[...], g.astype(BF16), preferred_element_type=F32) + bg_ref[...]
        s5 = (g * _sigmoid(z)).astype(BF16)
        acc = (lax.dot_general(s5, wo1_ref[...], _TN, preferred_element_type=F32)
               + jnp.dot(yml_ref[ts, :], wo2_ref[...], preferred_element_type=F32))
        x1_ref[ts, :] = x_ref[ts, :] + _rms(acc) * gpost_ref[...]


def _mixout(x2, yt, yml, w_glu, b_glu, w_out, g_post, tn):
    T = x2.shape[0]
    wgt = w_glu.T.astype(BF16)
    bg = b_glu.astype(F32).reshape(D_S5, 1)
    wo1 = w_out[:D_S5].astype(BF16)
    wo2 = w_out[D_S5:].astype(BF16)
    full = lambda shape: pl.BlockSpec(shape, lambda i: (0,) * len(shape))
    return pl.pallas_call(
        _mixout_kernel,
        grid=(T // tn,),
        in_specs=[
            pl.BlockSpec((tn, D_MODEL), lambda i: (i, 0)),
            pl.BlockSpec((tn // S5_CHUNK, D_S5, S5_CHUNK), lambda i: (i, 0, 0)),
            pl.BlockSpec((tn, D_ML), lambda i: (i, 0)),
            full((D_S5, D_S5)), full((D_S5, 1)),
            full((D_S5, D_MODEL)), full((D_ML, D_MODEL)),
            full((1, D_MODEL)),
        ],
        out_specs=pl.BlockSpec((tn, D_MODEL), lambda i: (i, 0)),
        out_shape=jax.ShapeDtypeStruct((T, D_MODEL), F32),
        compiler_params=_params(dimension_semantics=("arbitrary",)),
        name="mixout",
    )(x2, yt, yml, wgt, bg, wo1, wo2, g_post.astype(F32).reshape(1, D_MODEL))


def _ffn_kernel(x1_ref, gpre_ref, wg_ref, wu_ref, wd_ref, gn_ref, o_ref):
    h = (_rms(x1_ref[...]) * gpre_ref[...]).astype(BF16)
    acc = jnp.zeros((h.shape[0], D_MODEL), F32)
    for c in range(D_FF // FF_CHUNK):
        sl = slice(c * FF_CHUNK, (c + 1) * FF_CHUNK)
        gg = jnp.dot(h, wg_ref[:, sl], preferred_element_type=F32)
        uu = jnp.dot(h, wu_ref[:, sl], preferred_element_type=F32)
        a = (gg * _sigmoid(gg) * uu).astype(BF16)
        acc = acc + jnp.dot(a, wd_ref[sl, :], preferred_element_type=F32)
    o_ref[...] = x1_ref[...] + _rms(acc) * gn_ref[...]


def _ffn(x1, g_pre, w_gate, w_up, w_down, g_post, tm):
    T = x1.shape[0]
    const = lambda shape: pl.BlockSpec(shape, lambda i: (0,) * len(shape), pipeline_mode=pl.Buffered(1))
    return pl.pallas_call(
        _ffn_kernel,
        grid=(T // tm,),
        in_specs=[
            pl.BlockSpec((tm, D_MODEL), lambda i: (i, 0)),
            const((1, D_MODEL)),
            const((D_MODEL, D_FF)), const((D_MODEL, D_FF)), const((D_FF, D_MODEL)),
            const((1, D_MODEL)),
        ],
        out_specs=pl.BlockSpec((tm, D_MODEL), lambda i: (i, 0)),
        out_shape=jax.ShapeDtypeStruct((T, D_MODEL), F32),
        compiler_params=_params(dimension_semantics=("arbitrary",)),
        name="ffn",
    )(x1, g_pre.astype(F32).reshape(1, D_MODEL), w_gate.astype(BF16), w_up.astype(BF16), w_down.astype(BF16),
      g_post.astype(F32).reshape(1, D_MODEL))


def _tile(T, want):
    t = min(T, want)
    assert T % t == 0
    return t


def _layer(x, norm_mix_pre, norm_mix_post, norm_ffn_pre, norm_ffn_post, w_in, ml_gate_bias,
           s5_lam_re, s5_lam_im, s5_log_dt, s5_b_re, s5_b_im, s5_c_re, s5_c_im, s5_d,
           s5_w_glu, s5_b_glu, ml_conv_w, ml_conv_b, ml_wq, ml_wk, ml_wv, ml_head_norm,
           ml_skip, w_out, w_gate, w_up, w_down):
    B, S, D = x.shape
    assert D == D_MODEL and S % S5_CHUNK == 0 and S % ML_CHUNK == 0
    T = B * S
    nc5 = S // S5_CHUNK
    assert nc5 & (nc5 - 1) == 0, "the chunk scan assumes a power-of-two chunk count"
    x2 = x.reshape(T, D)

    ut, gt, xm, op = _inproj(x2, norm_mix_pre, w_in, _tile(T, 1024))

    kfull, w_e, w_c, apow = _s5_weights(s5_lam_re, s5_lam_im, s5_log_dt, s5_b_re, s5_b_im,
                                        s5_c_re, s5_c_im, s5_d, nc5)
    yt3 = _s5_core(ut.reshape(D_S5, T // S5_CHUNK, S5_CHUNK), kfull, w_e, w_c, apow, nc5)

    ncm = S // ML_CHUNK
    gt4 = jnp.transpose(gt.reshape(N_GATE_COLS, B, ncm, ML_CHUNK), (1, 2, 0, 3))
    yml = _mlstm(xm, op, gt4, ml_conv_w, ml_conv_b, ml_wq, ml_wk, ml_wv, ml_head_norm, ml_skip,
                 ml_gate_bias, B, S)

    x1 = _mixout(x2, yt3, yml, s5_w_glu, s5_b_glu, w_out, norm_mix_post, _tile(T, 1024))
    out = _ffn(x1, norm_ffn_pre, w_gate, w_up, w_down, norm_ffn_post, _tile(T, 512))
    return out.reshape(B, S, D)


def kernel(x, norm_mix_pre, norm_mix_post, norm_ffn_pre, norm_ffn_post, w_in, ml_gate_bias, s5_lam_re, s5_lam_im, s5_log_dt, s5_b_re, s5_b_im, s5_c_re, s5_c_im, s5_d, s5_w_glu, s5_b_glu, ml_conv_w, ml_conv_b, ml_wq, ml_wk, ml_wv, ml_head_norm, ml_skip, w_out, w_gate, w_up, w_down):
    depth = w_in.shape[0]
    for l in range(depth):
        x = _layer(
            x, norm_mix_pre[l], norm_mix_post[l], norm_ffn_pre[l], norm_ffn_post[l], w_in[l],
            ml_gate_bias[l], s5_lam_re[l], s5_lam_im[l], s5_log_dt[l], s5_b_re[l], s5_b_im[l],
            s5_c_re[l], s5_c_im[l], s5_d[l], s5_w_glu[l], s5_b_glu[l], ml_conv_w[l], ml_conv_b[l],
            ml_wq[l], ml_wk[l], ml_wv[l], ml_head_norm[l], ml_skip[l], w_out[l], w_gate[l],
            w_up[l], w_down[l])
    return x
```
